```python
import math, functools
import jax, jax.numpy as jnp
from jax import lax
import numpy as np

D_MODEL = 1024
BATCH = 1
SEQ = 16384
DEPTH = 2
DEC_BATCH = 32
DEC_SEQ = 8
PAST_LEN = 16384
PAGE_SIZE = 128

N_A_LAYERS = DEPTH // 2
N_B_LAYERS = DEPTH - N_A_LAYERS
GDN_HEADS = 8
GDN_DK = 128
GDN_DV = 128
GDN_CONV = 4
GDN_CHUNK = 64
GDN_QKV = GDN_HEADS * (2 * GDN_DK + GDN_DV)
GDN_PROJ = GDN_QKV + GDN_HEADS * GDN_DV + 2 * GDN_HEADS
NSA_HEADS = 16
NSA_KV_HEADS = 4
NSA_GROUP = NSA_HEADS // NSA_KV_HEADS
NSA_HD = D_MODEL // NSA_HEADS
NSA_QPROJ = NSA_HEADS * NSA_HD + 3 * NSA_HEADS
NSA_KVPROJ = 3 * 2 * NSA_KV_HEADS * NSA_HD
CMP_BLOCK = 32
CMP_STRIDE = 16
CMP_HID = NSA_HD
SEL_BLOCK = 64
SEL_TOP = 16
SEL_FORCE = 1000.0
WINDOW = 512
Q_BLOCK = 128
N_EXPERTS = 32
TOP_K = 4
D_FF = D_MODEL
SWIGLU_LIMIT = 7.0
SWIGLU_ALPHA = 1.702
MOE_BLOCK = 128
EPS = 1e-6
NEG = -1e30

kernel_name = "yoco_gdn_nsa_moe_step"

F32 = jnp.float32


def rmsnorm(x, g):
    xf = x.astype(F32)
    y = xf * lax.rsqrt(jnp.mean(xf * xf, -1, keepdims=True) + EPS)
    return (y * g.astype(F32)).astype(x.dtype)


def l2norm(x):
    xf = x.astype(F32)
    return xf * lax.rsqrt(jnp.sum(xf * xf, -1, keepdims=True) + EPS)


def modulate(x, shift, scale):
    return x * (1 + scale[:, None]) + shift[:, None]


def causal_conv(u, prev, w):
    full = jnp.concatenate([prev.astype(u.dtype), u], 1)
    L = u.shape[1]
    out = sum(full[:, j:j + L] * w[j] for j in range(GDN_CONV))
    return out, full[:, L:]


def gated_delta_rule(q, k, v, g, beta, s0):
    B, L, H, DK = q.shape
    DV = v.shape[-1]
    C = min(GDN_CHUNK, L)
    pad = (-L) % C
    n = (L + pad) // C

    def prep(t):
        t = jnp.pad(t.astype(F32), [(0, 0), (0, pad)] + [(0, 0)] * (t.ndim - 2))
        t = t.reshape((B, n, C) + t.shape[2:])
        return jnp.moveaxis(t, 3, 1)

    q = prep(q) * DK ** -0.5
    k, v, g, beta = prep(k), prep(v), prep(g), prep(beta)
    gc = jnp.cumsum(g, -1)
    incl = jnp.tril(jnp.ones((C, C), bool))
    strict = jnp.tril(jnp.ones((C, C), bool), -1)
    diff = gc[..., :, None] - gc[..., None, :]
    decay = jnp.where(incl, jnp.exp(jnp.where(incl, diff, 0.0)), 0.0)
    kb = k * beta[..., None]
    vb = v * beta[..., None]
    m = jnp.where(strict, jnp.einsum('bhncd,bhnsd->bhncs', kb, k) * decay, 0.0)
    eye = jnp.eye(C, dtype=F32)
    t_inv = lax.linalg.triangular_solve(m + eye, jnp.broadcast_to(eye, m.shape), left_side=True, lower=True)
    u = t_inv @ vb
    w = t_inv @ (kb * jnp.exp(gc)[..., None])
    attn = jnp.where(incl, jnp.einsum('bhncd,bhnsd->bhncs', q, k) * decay, 0.0)
    qg = q * jnp.exp(gc)[..., None]
    kg = k * jnp.exp(gc[..., -1:] - gc)[..., None]
    glast = jnp.exp(gc[..., -1])

    def step(s, xs):
        u_i, w_i, qg_i, kg_i, attn_i, gl_i = xs
        v_new = u_i - w_i @ s
        o = qg_i @ s + attn_i @ v_new
        s = s * gl_i[..., None, None] + jnp.swapaxes(kg_i, -1, -2) @ v_new
        return s, o

    xs = tuple(jnp.moveaxis(t, 2, 0) for t in (u, w, qg, kg, attn, glast))
    s_fin, o = lax.scan(step, s0.astype(F32), xs)
    o = jnp.moveaxis(o, 0, 2).reshape(B, H, n * C, DV)[:, :, :L]
    return jnp.swapaxes(o, 1, 2), s_fin


def gdn_mixer(h, s0, conv0, w_in, conv_w, a_log, dt_bias, norm_g, w_out):
    B, L, _ = h.shape
    proj = h @ w_in
    qkv, z, b, a = jnp.split(proj, [GDN_QKV, GDN_QKV + GDN_HEADS * GDN_DV,
                                    GDN_QKV + GDN_HEADS * GDN_DV + GDN_HEADS], -1)
    qkv_c, conv_new = causal_conv(qkv, conv0, conv_w)
    qkv_c = jax.nn.silu(qkv_c)
    q, k, v = jnp.split(qkv_c, [GDN_HEADS * GDN_DK, 2 * GDN_HEADS * GDN_DK], -1)
    q = l2norm(q.reshape(B, L, GDN_HEADS, GDN_DK))
    k = l2norm(k.reshape(B, L, GDN_HEADS, GDN_DK))
    v = v.reshape(B, L, GDN_HEADS, GDN_DV)
    beta = jax.nn.sigmoid(b.astype(F32))
    g = -jnp.exp(a_log.astype(F32)) * jax.nn.softplus(a.astype(F32) + dt_bias.astype(F32))
    o, s_new = gated_delta_rule(q, k, v, g, beta, s0)
    o = rmsnorm(o, norm_g) * jax.nn.silu(z.reshape(B, L, GDN_HEADS, GDN_DV).astype(F32))
    return o.reshape(B, L, -1).astype(h.dtype) @ w_out, s_new.astype(h.dtype), conv_new


def compress(x, pe, w1, w2):
    B, L = x.shape[:2]
    r = CMP_BLOCK // CMP_STRIDE
    nsub = L // CMP_STRIDE
    nc = nsub - r + 1
    sub = x[:, :nsub * CMP_STRIDE].reshape(B, nsub, CMP_STRIDE, NSA_KV_HEADS, NSA_HD).astype(F32)
    w1r = w1.astype(F32).reshape(r, CMP_STRIDE, NSA_HD, CMP_HID)
    per = jnp.einsum('bnjkd,ojde->obnke', sub, w1r)
    pre = sum(per[o, :, o:o + nc] for o in range(r)) + jnp.einsum('jd,jde->e', pe.astype(F32), w1.astype(F32))
    return jax.nn.silu(pre) @ w2.astype(F32)


def alibi_slopes():
    return jnp.asarray(2.0 ** (-8.0 * (np.arange(NSA_HEADS) + 1) / NSA_HEADS), F32)


def selection_map(nc, ns):
    r = CMP_BLOCK // CMP_STRIDE
    rs = SEL_BLOCK // CMP_STRIDE
    d = np.arange(nc)[:, None] - rs * np.arange(ns)[None, :]
    m = sum(((d + n >= 0) & (d + n < rs)).astype(np.float32) for n in range(r))
    return jnp.asarray(m, F32)


def nsa_attention(q, gates, q_start, kc, vc, ks, vs, kw, vw, w_start):
    B, T = q.shape[:2]
    KVH, G, HD = NSA_KV_HEADS, NSA_GROUP, NSA_HD
    nc, ls = kc.shape[1], ks.shape[1]
    ns = -(-ls // SEL_BLOCK)
    n_top = min(SEL_TOP, ns)
    qb = Q_BLOCK if T % Q_BLOCK == 0 else T
    nb = T // qb
    slope = alibi_slopes().reshape(KVH, G)[None, None, :, :, None]
    sel_map = selection_map(nc, ns)
    ci = jnp.arange(nc)
    c_end = ci * CMP_STRIDE + CMP_BLOCK - 1
    c_ctr = (ci * CMP_STRIDE).astype(F32) + 0.5 * (CMP_BLOCK - 1)
    kc, vc = kc.astype(F32), vc.astype(F32)
    ksT = jnp.swapaxes(ks.astype(F32), 1, 2)
    vsT = jnp.swapaxes(vs.astype(F32), 1, 2)
    wpad = [(0, 0), (WINDOW, 0), (0, 0), (0, 0)]
    kw_p = jnp.pad(kw.astype(F32), wpad)
    vw_p = jnp.pad(vw.astype(F32), wpad)
    lw = WINDOW + qb
    gather_rows = jax.vmap(jax.vmap(lambda a, i: a[i]))
    sj = jnp.arange(ns)

    def block(args):
        qi, gi, bi = args
        t0 = q_start + bi * qb
        tpos = t0 + jnp.arange(qb)
        tf = tpos.astype(F32)
        qg = qi.astype(F32).reshape(B, qb, KVH, G, HD) * HD ** -0.5
        s_c = jnp.einsum('btkgd,bnkd->btkgn', qg, kc) - slope * (tf[:, None] - c_ctr)[None, :, None, None, :]
        ok_c = (c_end[None, :] <= tpos[:, None])[None, :, None, None, :]
        p_c = jax.nn.softmax(jnp.where(ok_c, s_c, NEG), -1)
        p_c = jnp.where(jnp.any(ok_c, -1, keepdims=True), p_c, 0.0)
        o_c = jnp.einsum('btkgn,bnkd->btkgd', p_c, vc)
        imp = jnp.einsum('btkgn,ns->btks', p_c, sel_map)
        cur = (tpos // SEL_BLOCK)[None, :, None, None]
        forced = (sj == 0) | (sj == cur) | (sj == cur - 1)
        score = jnp.where(sj <= cur, jnp.where(forced, SEL_FORCE, imp), -1.0)
        top_s, top_j = lax.top_k(score, n_top)
        rows = top_j[..., None] * SEL_BLOCK + jnp.arange(SEL_BLOCK)
        ok_s = ((top_s >= 0.0)[..., None] & (rows <= tpos[None, :, None, None, None])).reshape(B, qb, KVH, 1, -1)
        idx = jnp.minimum(rows, ls - 1).transpose(0, 2, 1, 3, 4).reshape(B, KVH, -1)
        k_sel = gather_rows(ksT, idx).reshape(B, KVH, qb, n_top * SEL_BLOCK, HD)
        v_sel = gather_rows(vsT, idx).reshape(B, KVH, qb, n_top * SEL_BLOCK, HD)
        dist_s = (tpos[None, :, None, None, None] - rows).astype(F32).reshape(B, qb, KVH, 1, -1)
        s_s = jnp.einsum('btkgd,bktnd->btkgn', qg, k_sel) - slope * dist_s
        p_s = jax.nn.softmax(jnp.where(ok_s, s_s, NEG), -1)
        o_s = jnp.einsum('btkgn,bktnd->btkgd', p_s, v_sel)
        p0 = t0 - w_start
        kwi = lax.dynamic_slice_in_dim(kw_p, p0, lw, axis=1)
        vwi = lax.dynamic_slice_in_dim(vw_p, p0, lw, axis=1)
        wpos = t0 - WINDOW + jnp.arange(lw)
        s_w = jnp.einsum('btkgd,bskd->btkgs', qg, kwi) - slope * (tf[:, None] - wpos.astype(F32))[None, :, None, None, :]
        ok_w = (wpos >= 0) & (wpos <= tpos[:, None]) & (wpos > tpos[:, None] - WINDOW)
        p_w = jax.nn.softmax(jnp.where(ok_w[None, :, None, None, :], s_w, NEG), -1)
        o_w = jnp.einsum('btkgs,bskd->btkgd', p_w, vwi)
        gg = gi.astype(F32).reshape(B, qb, 3, KVH, G)[..., None]
        o = gg[:, :, 0] * o_c + gg[:, :, 1] * o_s + gg[:, :, 2] * o_w
        return o.reshape(B, qb, NSA_HEADS * HD)

    qs = jnp.moveaxis(q.reshape(B, nb, qb, NSA_HEADS, HD), 1, 0)
    gs = jnp.moveaxis(gates.reshape(B, nb, qb, 3, NSA_HEADS), 1, 0)
    out = lax.map(block, (qs, gs, jnp.arange(nb)))
    return jnp.moveaxis(out, 0, 1).reshape(B, T, NSA_HEADS * HD)


def moe(x, router_w, router_b, w_gu, b_gu, w_dn, b_dn):
    B, L, D = x.shape
    xt = x.reshape(-1, D)
    T = xt.shape[0]
    logits = (xt @ router_w + router_b).astype(F32)
    top_v, top_e = lax.top_k(logits, TOP_K)
    gate = jax.nn.softmax(top_v, -1)
    M = T * TOP_K
    flat_e = top_e.reshape(-1)
    order = jnp.argsort(flat_e)
    se = flat_e[order]
    counts = jnp.bincount(flat_e, length=N_EXPERTS)
    padded = (counts + MOE_BLOCK - 1) // MOE_BLOCK * MOE_BLOCK
    start = jnp.cumsum(counts) - counts
    pend = jnp.cumsum(padded)
    pstart = pend - padded
    dest = pstart[se] + jnp.arange(M) - start[se]
    nb = -(-(M + N_EXPERTS * (MOE_BLOCK - 1)) // MOE_BLOCK)
    R = nb * MOE_BLOCK
    row_tok = jnp.full((R,), T, jnp.int32).at[dest].set((order // TOP_K).astype(jnp.int32))
    row_w = jnp.zeros((R,), F32).at[dest].set(gate.reshape(-1)[order])
    blk_e = jnp.minimum(jnp.searchsorted(pend, jnp.arange(nb) * MOE_BLOCK, side='right'), N_EXPERTS - 1)
    x_pad = jnp.concatenate([xt, jnp.zeros((1, D), xt.dtype)], 0)
    xb = x_pad[row_tok].reshape(nb, MOE_BLOCK, D)

    def expert(args):
        xe, e = args
        gu = xe @ w_gu[e] + b_gu[e]
        g, u = gu[..., ::2], gu[..., 1::2]
        g = jnp.minimum(g, SWIGLU_LIMIT)
        u = jnp.clip(u, -SWIGLU_LIMIT, SWIGLU_LIMIT)
        return ((u + 1) * (g * jax.nn.sigmoid(SWIGLU_ALPHA * g))) @ w_dn[e] + b_dn[e]

    yb = lax.map(expert, (xb, blk_e)).reshape(R, D)
    y = jnp.zeros((T + 1, D), F32).at[row_tok].add(yb.astype(F32) * row_w[:, None])
    return y[:T].reshape(B, L, D).astype(x.dtype)


def trunk(x, c, gdn_s0, conv_s0, past_cmp, past_slc, win_buf,
          ada_w, ada_b, norm_mix, norm_ffn, gdn_w_in, gdn_conv_w, gdn_a_log, gdn_dt_bias, gdn_norm,
          gdn_w_out, kv_ada_w, kv_ada_b, kv_norm, kv_w, cmp_pe, cmp_w1, cmp_w2, nsa_w_in, nsa_w_out,
          router_w, router_b, moe_w_gu, moe_b_gu, moe_w_dn, moe_b_dn, norm_f):
    B, L, _ = x.shape
    past = past_slc.shape[1]
    h = x
    cs = jax.nn.silu(c)
    gdn_states, conv_states = [], []
    for l in range(DEPTH):
        sh1, sc1, g1, sh2, sc2, g2 = jnp.split(cs @ ada_w[l] + ada_b[l], 6, -1)
        if l == N_A_LAYERS:
            kv_sh, kv_sc = jnp.split(cs @ kv_ada_w + kv_ada_b, 2, -1)
            kv = (modulate(rmsnorm(h, kv_norm), kv_sh, kv_sc) @ kv_w).reshape(B, L, 3, 2, NSA_KV_HEADS, NSA_HD)
            cmp_new, slc_new, win_new = kv[:, :, 0], kv[:, :, 1], kv[:, :, 2]
            cmp_all = jnp.concatenate([past_cmp.astype(kv.dtype), cmp_new], 1)
            slc_all = jnp.concatenate([past_slc.astype(kv.dtype), slc_new], 1)
            win_all = jnp.concatenate([win_buf.astype(kv.dtype), win_new], 1)
            kc = compress(cmp_all[:, :, 0], cmp_pe[0], cmp_w1[0], cmp_w2[0])
            vc = compress(cmp_all[:, :, 1], cmp_pe[1], cmp_w1[1], cmp_w2[1])
            w_start = past - win_buf.shape[1]
            win_state = win_all[:, max(0, win_all.shape[1] - WINDOW):]
        u = modulate(rmsnorm(h, norm_mix[l]), sh1, sc1)
        if l < N_A_LAYERS:
            mix, s_new, cv_new = gdn_mixer(u, gdn_s0[l], conv_s0[l], gdn_w_in[l], gdn_conv_w[l], gdn_a_log[l],
                                           gdn_dt_bias[l], gdn_norm[l], gdn_w_out[l])
            gdn_states.append(s_new)
            conv_states.append(cv_new)
        else:
            j = l - N_A_LAYERS
            proj = u @ nsa_w_in[j]
            qh = proj[..., :NSA_HEADS * NSA_HD].reshape(B, L, NSA_HEADS, NSA_HD)
            gates = jax.nn.sigmoid(proj[..., NSA_HEADS * NSA_HD:].astype(F32)).reshape(B, L, 3, NSA_HEADS)
            o = nsa_attention(qh, gates, past, kc, vc, slc_all[:, :, 0], slc_all[:, :, 1],
                              win_all[:, :, 0], win_all[:, :, 1], w_start)
            mix = o.astype(h.dtype) @ nsa_w_out[j]
        h = h + g1[:, None] * mix
        u = modulate(rmsnorm(h, norm_ffn[l]), sh2, sc2)
        h = h + g2[:, None] * moe(u, router_w[l], router_b[l], moe_w_gu[l], moe_b_gu[l], moe_w_dn[l], moe_b_dn[l])
    y = rmsnorm(h, norm_f)
    return y, jnp.stack(gdn_states), jnp.stack(conv_states), cmp_new, slc_new, win_state


def setup_inputs(seed: int = 0) -> dict:
    key = jax.random.key(seed)
    keys = iter(jax.random.split(key, 64))

    def nrm(shape, scale):
        return jax.random.normal(next(keys), shape, F32) * scale

    D = D_MODEL
    n_pages = PAST_LEN // PAGE_SIZE
    n_used = DEC_BATCH * n_pages
    n_pool = (5 * n_used + 3) // 4
    w_buf = min(WINDOW, PAST_LEN)
    page_table = jax.random.permutation(next(keys), n_pool)[:n_used].astype(jnp.int32).reshape(DEC_BATCH, n_pages)
    a_log = jnp.log(jax.random.uniform(next(keys), (N_A_LAYERS, GDN_HEADS), F32, 1.0, 16.0))
    dt = jnp.exp(jax.random.uniform(next(keys), (N_A_LAYERS, GDN_HEADS), F32, math.log(1e-3), math.log(1e-1)))
    dt_bias = dt + jnp.log(-jnp.expm1(-dt))
    return {
        "x_prompt": nrm((BATCH, SEQ, D), 1.0),
        "x_sample": nrm((DEC_BATCH, DEC_SEQ, D), 1.0),
        "state_gdn": nrm((N_A_LAYERS, DEC_BATCH, GDN_HEADS, GDN_DK, GDN_DV), 0.3),
        "state_conv": nrm((N_A_LAYERS, DEC_BATCH, GDN_CONV - 1, GDN_QKV), 1.0),
        "cache_cmp_kv": nrm((n_pool, PAGE_SIZE, 2, NSA_KV_HEADS, NSA_HD), 1.0),
        "cache_slc_kv": nrm((n_pool, PAGE_SIZE, 2, NSA_KV_HEADS, NSA_HD), 1.0),
        "cache_win_kv": nrm((DEC_BATCH, w_buf, 2, NSA_KV_HEADS, NSA_HD), 1.0),
        "page_table": page_table,
        "c_prompt": nrm((BATCH, D), 1.0),
        "c_sample": nrm((DEC_BATCH, D), 1.0),
        "ada_w": nrm((DEPTH, D, 6 * D), 0.5 * D ** -0.5),
        "ada_b": nrm((DEPTH, 6 * D), 0.02),
        "norm_mix": 1.0 + nrm((DEPTH, D), 0.02),
        "norm_ffn": 1.0 + nrm((DEPTH, D), 0.02),
        "gdn_w_in": nrm((N_A_LAYERS, D, GDN_PROJ), D ** -0.5),
        "gdn_conv_w": nrm((N_A_LAYERS, GDN_CONV, GDN_QKV), GDN_CONV ** -0.5),
        "gdn_a_log": a_log,
        "gdn_dt_bias": dt_bias,
        "gdn_norm": 1.0 + nrm((N_A_LAYERS, GDN_DV), 0.02),
        "gdn_w_out": nrm((N_A_LAYERS, GDN_HEADS * GDN_DV, D), (GDN_HEADS * GDN_DV) ** -0.5),
        "kv_ada_w": nrm((D, 2 * D), 0.5 * D ** -0.5),
        "kv_ada_b": nrm((2 * D,), 0.02),
        "kv_norm": 1.0 + nrm((D,), 0.02),
        "kv_w": nrm((D, NSA_KVPROJ), D ** -0.5),
        "cmp_pe": nrm((2, CMP_BLOCK, NSA_HD), 0.1),
        "cmp_w1": nrm((2, CMP_BLOCK, NSA_HD, CMP_HID), (CMP_BLOCK * NSA_HD) ** -0.5),
        "cmp_w2": nrm((2, CMP_HID, NSA_HD), CMP_HID ** -0.5),
        "nsa_w_in": nrm((N_B_LAYERS, D, NSA_QPROJ), D ** -0.5),
        "nsa_w_out": nrm((N_B_LAYERS, NSA_HEADS * NSA_HD, D), (NSA_HEADS * NSA_HD) ** -0.5),
        "router_w": nrm((DEPTH, D, N_EXPERTS), D ** -0.5),
        "router_b": nrm((DEPTH, N_EXPERTS), 0.01),
        "moe_w_gu": nrm((DEPTH, N_EXPERTS, D, 2 * D_FF), D ** -0.5),
        "moe_b_gu": nrm((DEPTH, N_EXPERTS, 2 * D_FF), 0.01),
        "moe_w_dn": nrm((DEPTH, N_EXPERTS, D_FF, D), D_FF ** -0.5),
        "moe_b_dn": nrm((DEPTH, N_EXPERTS, D), 0.01),
        "norm_f": 1.0 + nrm((D,), 0.02),
    }


def reference(x_prompt, x_sample, state_gdn, state_conv, cache_cmp_kv, cache_slc_kv, cache_win_kv, page_table,
              c_prompt, c_sample, ada_w, ada_b, norm_mix, norm_ffn, gdn_w_in, gdn_conv_w, gdn_a_log, gdn_dt_bias,
              gdn_norm, gdn_w_out, kv_ada_w, kv_ada_b, kv_norm, kv_w, cmp_pe, cmp_w1, cmp_w2, nsa_w_in, nsa_w_out,
              router_w, router_b, moe_w_gu, moe_b_gu, moe_w_dn, moe_b_dn, norm_f):
    weights = (ada_w, ada_b, norm_mix, norm_ffn, gdn_w_in, gdn_conv_w, gdn_a_log, gdn_dt_bias, gdn_norm,
               gdn_w_out, kv_ada_w, kv_ada_b, kv_norm, kv_w, cmp_pe, cmp_w1, cmp_w2, nsa_w_in, nsa_w_out,
               router_w, router_b, moe_w_gu, moe_b_gu, moe_w_dn, moe_b_dn, norm_f)
    bp = x_prompt.shape[0]
    dt = x_prompt.dtype
    p_gdn0 = jnp.zeros((N_A_LAYERS, bp, GDN_HEADS, GDN_DK, GDN_DV), dt)
    p_conv0 = jnp.zeros((N_A_LAYERS, bp, GDN_CONV - 1, GDN_QKV), dt)
    empty = jnp.zeros((bp, 0, 2, NSA_KV_HEADS, NSA_HD), dt)
    y_prompt, p_gdn, p_conv, p_cmp, p_slc, p_win = trunk(
        x_prompt, c_prompt, p_gdn0, p_conv0, empty, empty, empty, *weights)
    db = page_table.shape[0]
    past_cmp = cache_cmp_kv[page_table].reshape((db, -1) + cache_cmp_kv.shape[2:])
    past_slc = cache_slc_kv[page_table].reshape((db, -1) + cache_slc_kv.shape[2:])
    y_sample, s_gdn, s_conv, s_cmp, s_slc, s_win = trunk(
        x_sample, c_sample, state_gdn, state_conv, past_cmp, past_slc, cache_win_kv, *weights)
    return (y_prompt, y_sample, p_gdn, p_conv, p_cmp, p_slc, p_win, s_gdn, s_conv, s_cmp, s_slc, s_win)
```

```python
import functools

import jax
import jax.numpy as jnp
import numpy as np
from jax import lax
from jax.experimental import pallas as pl
from jax.experimental.pallas import tpu as pltpu

F32 = jnp.float32
BF16 = jnp.bfloat16
I32 = jnp.int32

GDN_HEADS = 8
GDN_DK = 128
GDN_DV = 128
GDN_CONV = 4
GDN_CHUNK = 64
NSA_HEADS = 16
NSA_KV_HEADS = 4
NSA_GROUP = NSA_HEADS // NSA_KV_HEADS
NSA_HD = 64
CMP_BLOCK = 32
CMP_STRIDE = 16
SEL_BLOCK = 64
SEL_TOP = 16
SEL_FORCE = 1000.0
WINDOW = 512
TOP_K = 4
SWIGLU_LIMIT = 7.0
SWIGLU_ALPHA = 1.702
EPS = 1e-6
NEG = -1e30

LANES = 128
SUBLANES = 8
ROW_TILE = 256
MOE_TILE = 256
KV_TILE = 512
Q_TILE = 128
CMP_PAGES = 16
SLC_PAGES = 8
VMEM_LIMIT = 56 * 1024 * 1024
HIGHEST = lax.Precision.HIGHEST


def _cparams(sem):
    return pltpu.CompilerParams(dimension_semantics=sem, vmem_limit_bytes=VMEM_LIMIT)


def _mm(a, b):
    return jnp.dot(a.astype(BF16), b.astype(BF16), preferred_element_type=F32)


def _mm_nt(a, b):
    return lax.dot_general(a.astype(BF16), b.astype(BF16), (((1,), (1,)), ((), ())),
                           preferred_element_type=F32)


def _mm_tn(a, b):
    return lax.dot_general(a.astype(BF16), b.astype(BF16), (((0,), (0,)), ((), ())),
                           preferred_element_type=F32)


def _mm32(a, b):
    return jnp.dot(a, b, precision=HIGHEST, preferred_element_type=F32)


def _mm_split(a_exact_bf16, p):
    p_hi = p.astype(BF16)
    p_lo = (p - p_hi.astype(F32)).astype(BF16)
    return (jnp.dot(a_exact_bf16, p_hi, preferred_element_type=F32)
            + jnp.dot(a_exact_bf16, p_lo, preferred_element_type=F32))


def _mm_split_l(p, b_exact_bf16):
    p_hi = p.astype(BF16)
    p_lo = (p - p_hi.astype(F32)).astype(BF16)
    return (jnp.dot(p_hi, b_exact_bf16, preferred_element_type=F32)
            + jnp.dot(p_lo, b_exact_bf16, preferred_element_type=F32))


def _row_to_col(row):
    n = row.shape[1]
    eye = _iota((n, n), 0) == _iota((n, n), 1)
    return jnp.sum(jnp.where(eye, jnp.broadcast_to(row, (n, n)), 0.0), axis=1, keepdims=True)


def _silu(x):
    return x * jax.nn.sigmoid(x)


def _iota(shape, dim):
    return lax.broadcasted_iota(I32, shape, dim)


def _cond_kernel(c_ref, w_ref, b_ref, o_ref):
    o_ref[...] = _mm(_silu(c_ref[...]), w_ref[...]) + b_ref[...]


def cond_matmul(c, w, b, tn=1024):
    m, d = c.shape
    n = w.shape[1]
    return pl.pallas_call(
        _cond_kernel,
        grid=(n // tn,),
        in_specs=[pl.BlockSpec((m, d), lambda j: (0, 0)),
                  pl.BlockSpec((d, tn), lambda j: (0, j)),
                  pl.BlockSpec((1, tn), lambda j: (0, j))],
        out_specs=pl.BlockSpec((m, tn), lambda j: (0, j)),
        out_shape=jax.ShapeDtypeStruct((m, n), F32),
        compiler_params=_cparams(("arbitrary",)),
        name="cond_matmul",
    )(c, w, b.reshape(1, n))


def _mod_map(n_prompt_tiles):
    return lambda i: (jnp.where(i < n_prompt_tiles, 0, 1), 0)


def _norm_proj_kernel(n_heads, x_ref, *refs):
    x = x_ref[...]
    xn = x * lax.rsqrt(jnp.mean(x * x, -1, keepdims=True) + EPS)
    for i in range(n_heads):
        g_ref, sh_ref, sc_ref, w_ref = refs[4 * i:4 * i + 4]
        o_ref = refs[4 * n_heads + i]
        u = (xn * g_ref[...]) * (1.0 + sc_ref[...]) + sh_ref[...]
        o_ref[...] = _mm(u, w_ref[...])


def norm_proj(h, heads, n_prompt_tiles):
    t, d = h.shape
    tm = ROW_TILE
    in_specs = [pl.BlockSpec((tm, d), lambda i: (i, 0))]
    args = [h]
    out_specs, out_shapes = [], []
    for gamma, sh, sc, w in heads:
        n = w.shape[1]
        in_specs += [pl.BlockSpec((1, d), lambda i: (0, 0)),
                     pl.BlockSpec((tm, d), _mod_map(n_prompt_tiles)),
                     pl.BlockSpec((tm, d), _mod_map(n_prompt_tiles)),
                     pl.BlockSpec((d, n), lambda i: (0, 0))]
        args += [gamma.reshape(1, d), sh, sc, w]
        out_specs.append(pl.BlockSpec((tm, n), lambda i: (i, 0)))
        out_shapes.append(jax.ShapeDtypeStruct((t, n), F32))
    return pl.pallas_call(
        functools.partial(_norm_proj_kernel, len(heads)),
        grid=(t // tm,),
        in_specs=in_specs,
        out_specs=out_specs,
        out_shape=out_shapes,
        compiler_params=_cparams(("parallel",)),
        name="norm_proj",
    )(*args)


def _proj_res_kernel(a_ref, w_ref, h_ref, g_ref, o_ref):
    o_ref[...] = h_ref[...] + g_ref[...] * _mm(a_ref[...], w_ref[...])


def proj_residual(a, w, h, gate_rows, n_prompt_tiles):
    t, k = a.shape
    d = w.shape[1]
    tm = ROW_TILE
    return pl.pallas_call(
        _proj_res_kernel,
        grid=(t // tm,),
        in_specs=[pl.BlockSpec((tm, k), lambda i: (i, 0)),
                  pl.BlockSpec((k, d), lambda i: (0, 0)),
                  pl.BlockSpec((tm, d), lambda i: (i, 0)),
                  pl.BlockSpec((tm, d), _mod_map(n_prompt_tiles))],
        out_specs=pl.BlockSpec((tm, d), lambda i: (i, 0)),
        out_shape=jax.ShapeDtypeStruct((t, d), F32),
        compiler_params=_cparams(("parallel",)),
        name="proj_residual",
    )(a, w, h, gate_rows)


def _tri_inv(m, c):
    r = _iota((c, c), 0)
    col = _iota((c, c), 1)
    eye = (r == col).astype(F32)
    d = jnp.where((r // 8) == (col // 8), m, 0.0)
    d2 = _mm32(d, d)
    d4 = _mm32(d2, d2)
    x = eye - d
    x = x + _mm32(x, d2)
    x = x + _mm32(x, d4)
    size = 8
    while size < c:
        low = jnp.where(((r // (2 * size)) == (col // (2 * size))) & ((r // size) != (col // size)), m, 0.0)
        x = x - _mm32(_mm32(x, low), x)
        size *= 2
    return x


def _gdn_kernel(c, q_ref, k_ref, v_ref, z_ref, ba_ref, conv0_ref, s0_ref, cw_ref, alog_ref, dtb_ref, ng_ref,
                o_ref, sfin_ref, convn_ref, xs_ref, xc_ref, st_ref):
    n = pl.program_id(1)
    nh, dk = GDN_HEADS, GDN_DK
    hw = nh * dk

    @pl.when(n == 0)
    def _init():
        xs_ref[0:8, :] = conv0_ref[...]
        st_ref[...] = s0_ref[...]

    xs_ref[8:8 + c, 0:hw] = q_ref[...]
    xs_ref[8:8 + c, hw:2 * hw] = k_ref[...]
    xs_ref[8:8 + c, 2 * hw:3 * hw] = v_ref[...]
    acc = cw_ref[3:4, :] * xs_ref[8:8 + c, :]
    for j in range(GDN_CONV - 1):
        acc = acc + cw_ref[j:j + 1, :] * xs_ref[5 + j:5 + j + c, :]
    xc_ref[...] = _silu(acc)
    convn_ref[...] = xs_ref[5 + c:8 + c, :]
    xs_ref[0:8, :] = xs_ref[c:c + 8, :]

    ba = ba_ref[...]
    beta_all = jax.nn.sigmoid(ba)
    xa = ba + dtb_ref[...]
    softplus = jnp.maximum(xa, 0.0) + jnp.log(1.0 + jnp.exp(-jnp.abs(xa)))
    g_all = -jnp.exp(alog_ref[...]) * softplus

    r = _iota((c, c), 0)
    col = _iota((c, c), 1)
    incl = r >= col
    strict = r > col
    incl_f = incl.astype(F32)
    upper_f = (r <= col).astype(F32)
    ones_cc = jnp.ones((c, c), F32)

    for h in range(nh):
        qh = xc_ref[:, h * dk:(h + 1) * dk]
        kh = xc_ref[:, hw + h * dk:hw + (h + 1) * dk]
        vh = xc_ref[:, 2 * hw + h * dk:2 * hw + (h + 1) * dk]
        qn = qh * lax.rsqrt(jnp.sum(qh * qh, -1, keepdims=True) + EPS) * (dk ** -0.5)
        kn = kh * lax.rsqrt(jnp.sum(kh * kh, -1, keepdims=True) + EPS)
        beta = beta_all[:, h:h + 1]
        g_b = jnp.broadcast_to(g_all[:, nh + h:nh + h + 1], (c, dk))
        gc_b = _mm32(incl_f, g_b)
        gc_row = _mm32(ones_cc, g_b[:, 0:c] * upper_f)
        diff = gc_b[:, 0:c] - gc_row
        decay = jnp.where(incl, jnp.exp(jnp.where(incl, diff, 0.0)), 0.0)
        kb = kn * beta
        vb = vh * beta
        m = jnp.where(strict, _mm_nt(kb, kn) * decay, 0.0)
        t_inv = _tri_inv(m, c)
        egc = jnp.exp(gc_b)
        u = _mm(t_inv, vb)
        w = _mm(t_inv, kb * egc)
        attn = jnp.where(incl, _mm_nt(qn, kn) * decay, 0.0)
        qg = qn * egc
        gl_b = gc_b[c - 1:c, :]
        kg = kn * jnp.exp(gl_b - gc_b)
        s = st_ref[h]
        v_new = u - _mm(w, s)
        o = _mm(qg, s) + _mm(attn, v_new)
        st_ref[h] = s * jnp.exp(gl_b) + _mm_tn(kg, v_new)
        on = o * lax.rsqrt(jnp.mean(o * o, -1, keepdims=True) + EPS) * ng_ref[...]
        o_ref[:, h * dk:(h + 1) * dk] = on * _silu(z_ref[:, h * dk:(h + 1) * dk])

    sfin_ref[...] = st_ref[...]


def gdn(proj, row_off, b, l, c, s0, conv0, conv_w, a_log, dt_bias, norm_g):
    nh, dk = GDN_HEADS, GDN_DK
    hw = nh * dk
    nchunks = l // c
    rb0 = row_off // c
    row = lambda bi, n: rb0 + bi * nchunks + n
    gate_blk = 4 * hw // LANES
    lane_pad = jnp.zeros((LANES - 2 * nh,), F32)
    alog_row = jnp.concatenate([jnp.zeros((nh,), F32), a_log, lane_pad]).reshape(1, LANES)
    dtb_row = jnp.concatenate([jnp.zeros((nh,), F32), dt_bias, lane_pad]).reshape(1, LANES)
    return pl.pallas_call(
        functools.partial(_gdn_kernel, c),
        grid=(b, nchunks),
        in_specs=[pl.BlockSpec((c, hw), lambda bi, n: (row(bi, n), 0)),
                  pl.BlockSpec((c, hw), lambda bi, n: (row(bi, n), 1)),
                  pl.BlockSpec((c, hw), lambda bi, n: (row(bi, n), 2)),
                  pl.BlockSpec((c, hw), lambda bi, n: (row(bi, n), 3)),
                  pl.BlockSpec((c, LANES), lambda bi, n: (row(bi, n), gate_blk)),
                  pl.BlockSpec((None, 8, 3 * hw), lambda bi, n: (bi, 0, 0)),
                  pl.BlockSpec((None, nh, dk, GDN_DV), lambda bi, n: (bi, 0, 0, 0)),
                  pl.BlockSpec((GDN_CONV, 3 * hw), lambda bi, n: (0, 0)),
                  pl.BlockSpec((1, LANES), lambda bi, n: (0, 0)),
                  pl.BlockSpec((1, LANES), lambda bi, n: (0, 0)),
                  pl.BlockSpec((1, GDN_DV), lambda bi, n: (0, 0))],
        out_specs=[pl.BlockSpec((c, hw), lambda bi, n: (bi * nchunks + n, 0)),
                   pl.BlockSpec((None, nh, dk, GDN_DV), lambda bi, n: (bi, 0, 0, 0)),
                   pl.BlockSpec((None, GDN_CONV - 1, 3 * hw), lambda bi, n: (bi, 0, 0))],
        out_shape=[jax.ShapeDtypeStruct((b * l, hw), F32),
                   jax.ShapeDtypeStruct((b, nh, dk, GDN_DV), F32),
                   jax.ShapeDtypeStruct((b, GDN_CONV - 1, 3 * hw), F32)],
        scratch_shapes=[pltpu.VMEM((c + 8, 3 * hw), F32),
                        pltpu.VMEM((c, 3 * hw), F32),
                        pltpu.VMEM((nh, dk, GDN_DV), F32)],
        compiler_params=_cparams(("parallel", "arbitrary")),
        name="gdn",
    )(proj, proj, proj, proj, proj, conv0, s0, conv_w, alog_row, dtb_row, norm_g.reshape(1, GDN_DV))


def _route_kernel(x_ref, g_ref, sh_ref, sc_ref, rw_ref, rb_ref, u_ref, e_ref, w_ref):
    x = x_ref[...]
    xn = x * lax.rsqrt(jnp.mean(x * x, -1, keepdims=True) + EPS)
    u = (xn * g_ref[...]) * (1.0 + sc_ref[...]) + sh_ref[...]
    u_ref[...] = u
    logits = _mm32(u, rw_ref[...]) + rb_ref[...]
    tm, ne = logits.shape
    eidx = _iota((tm, ne), 1)
    lane = _iota((tm, LANES), 1)
    e_out = jnp.zeros((tm, LANES), I32)
    vals = []
    for k in range(TOP_K):
        m = jnp.max(logits, -1, keepdims=True)
        idx = jnp.min(jnp.where(logits == m, eidx, ne), -1, keepdims=True)
        logits = jnp.where(eidx == idx, -3e38, logits)
        vals.append(m)
        e_out = jnp.where(lane == k, idx, e_out)
    ex = [jnp.exp(v - vals[0]) for v in vals]
    den = ex[0] + ex[1] + ex[2] + ex[3]
    w_out = jnp.zeros((tm, LANES), F32)
    for k in range(TOP_K):
        w_out = jnp.where(lane == k, ex[k] / den, w_out)
    e_ref[...] = e_out
    w_ref[...] = w_out


def moe_route(h, gamma, sh, sc, router_w, router_b, n_prompt_tiles):
    t, d = h.shape
    ne = router_w.shape[1]
    tm = ROW_TILE
    return pl.pallas_call(
        _route_kernel,
        grid=(t // tm,),
        in_specs=[pl.BlockSpec((tm, d), lambda i: (i, 0)),
                  pl.BlockSpec((1, d), lambda i: (0, 0)),
                  pl.BlockSpec((tm, d), _mod_map(n_prompt_tiles)),
                  pl.BlockSpec((tm, d), _mod_map(n_prompt_tiles)),
                  pl.BlockSpec((d, ne), lambda i: (0, 0)),
                  pl.BlockSpec((1, ne), lambda i: (0, 0))],
        out_specs=[pl.BlockSpec((tm, d), lambda i: (i, 0)),
                   pl.BlockSpec((tm, LANES), lambda i: (i, 0)),
                   pl.BlockSpec((tm, LANES), lambda i: (i, 0))],
        out_shape=[jax.ShapeDtypeStruct((t, d), F32),
                   jax.ShapeDtypeStruct((t, LANES), I32),
                   jax.ShapeDtypeStruct((t, LANES), F32)],
        compiler_params=_cparams(("parallel",)),
        name="moe_route",
    )(h, gamma.reshape(1, d), sh, sc, router_w, router_b.reshape(1, ne))


def _expert_kernel(be_ref, nu_ref, tok0_ref, tokn_ref, dst_ref, x_hbm,
                   wg_ref, wu_ref, bg_ref, bu_ref, wd_ref, bd_ref,
                   out_hbm, xbuf, ybuf, gsem, ssem):
    i = pl.program_id(0)
    n_used = nu_ref[0]
    slot = lax.rem(i, 2)
    tm = xbuf.shape[1]

    def row_in(tok, s, j):
        return pltpu.make_async_copy(x_hbm.at[pl.ds(tok, 1)], xbuf.at[s, pl.ds(j, 1)], gsem.at[s])

    def row_out(dst, s, j):
        return pltpu.make_async_copy(ybuf.at[s, pl.ds(j, 1)], out_hbm.at[pl.ds(dst, 1)], ssem.at[s])

    def start_gather(tok_ref, s):
        def body(j, carry):
            row_in(tok_ref[0, j], s, j).start()
            return carry
        lax.fori_loop(0, tm, body, 0)

    def wait_gather(s):
        def body(j, carry):
            row_in(0, s, j).wait()
            return carry
        lax.fori_loop(0, tm, body, 0)

    def start_scatter(s):
        def body(j, carry):
            row_out(dst_ref[0, j], s, j).start()
            return carry
        lax.fori_loop(0, tm, body, 0)

    def wait_scatter(s):
        def body(j, carry):
            row_out(0, s, j).wait()
            return carry
        lax.fori_loop(0, tm, body, 0)

    @pl.when(i == 0)
    def _first():
        start_gather(tok0_ref, 0)
        ybuf[...] = jnp.zeros(ybuf.shape, F32)
        base = out_hbm.shape[0] - 2 * tm
        for s in range(2):
            fill = pltpu.make_async_copy(ybuf.at[s], out_hbm.at[pl.ds(base + s * tm, tm)], ssem.at[s])
            fill.start()
            fill.wait()

    @pl.when(i + 1 < n_used)
    def _prefetch():
        start_gather(tokn_ref, 1 - slot)

    @pl.when(i < n_used)
    def _block():
        wait_gather(slot)

        @pl.when(i >= 2)
        def _reuse():
            wait_scatter(slot)

        x = xbuf[slot]
        g = _mm(x, wg_ref[...]) + bg_ref[...]
        u = _mm(x, wu_ref[...]) + bu_ref[...]
        g = jnp.minimum(g, SWIGLU_LIMIT)
        u = jnp.clip(u, -SWIGLU_LIMIT, SWIGLU_LIMIT)
        a = (u + 1.0) * (g * jax.nn.sigmoid(SWIGLU_ALPHA * g))
        ybuf[slot] = _mm(a, wd_ref[...]) + bd_ref[...]
        start_scatter(slot)

        @pl.when(i == n_used - 1)
        def _drain():
            wait_scatter(slot)

            @pl.when(i >= 1)
            def _prev():
                wait_scatter(1 - slot)


def moe_experts(u, row_tok, row_dst, blk_e, n_used, wg, wu, bg, bu, wd, bd, n_out_rows):
    t, d = u.shape
    ne, _, f = wg.shape
    tm = MOE_TILE
    nb = row_tok.shape[0] // tm
    tok3 = row_tok.reshape(nb, 1, tm)
    dst3 = row_dst.reshape(nb, 1, tm)
    smem_blk = lambda imap: pl.BlockSpec((None, 1, tm), imap, memory_space=pltpu.SMEM)
    grid_spec = pltpu.PrefetchScalarGridSpec(
        num_scalar_prefetch=2,
        grid=(nb,),
        in_specs=[smem_blk(lambda i, be, nu: (i, 0, 0)),
                  smem_blk(lambda i, be, nu: (jnp.minimum(i + 1, nb - 1), 0, 0)),
                  smem_blk(lambda i, be, nu: (i, 0, 0)),
                  pl.BlockSpec(memory_space=pl.ANY),
                  pl.BlockSpec((None, d, f), lambda i, be, nu: (be[i], 0, 0)),
                  pl.BlockSpec((None, d, f), lambda i, be, nu: (be[i], 0, 0)),
                  pl.BlockSpec((None, 1, f), lambda i, be, nu: (be[i], 0, 0)),
                  pl.BlockSpec((None, 1, f), lambda i, be, nu: (be[i], 0, 0)),
                  pl.BlockSpec((None, f, d), lambda i, be, nu: (be[i], 0, 0)),
                  pl.BlockSpec((None, 1, d), lambda i, be, nu: (be[i], 0, 0))],
        out_specs=pl.BlockSpec(memory_space=pl.ANY),
        scratch_shapes=[pltpu.VMEM((2, tm, d), F32),
                        pltpu.VMEM((2, tm, d), F32),
                        pltpu.SemaphoreType.DMA((2,)),
                        pltpu.SemaphoreType.DMA((2,))],
    )
    return pl.pallas_call(
        _expert_kernel,
        grid_spec=grid_spec,
        out_shape=jax.ShapeDtypeStruct((n_out_rows, d), F32),
        compiler_params=_cparams(("arbitrary",)),
        name="moe_experts",
    )(blk_e, n_used, tok3, tok3, dst3, u, wg, wu, bg, bu, wd, bd)


def route_tables(top_e, n_exp):
    t = top_e.shape[0]
    tm = MOE_TILE
    m = t * TOP_K
    flat_e = top_e.reshape(-1)
    order = jnp.argsort(flat_e, stable=True).astype(I32)
    se = flat_e[order]
    counts = jnp.bincount(flat_e, length=n_exp).astype(I32)
    padded = (counts + tm - 1) // tm * tm
    start = jnp.cumsum(counts) - counts
    pend = jnp.cumsum(padded)
    pstart = pend - padded
    dest = pstart[se] + jnp.arange(m, dtype=I32) - start[se]
    nb = -(-(m + n_exp * (tm - 1)) // tm)
    rows = nb * tm
    tok = order // TOP_K
    slot = order % TOP_K
    ridx = jnp.arange(rows, dtype=I32)
    pad_dst = TOP_K * t + ((ridx // tm) % 2) * tm + ridx % tm
    row_tok = jnp.zeros((rows,), I32).at[dest].set(tok)
    row_dst = pad_dst.at[dest].set(slot * t + tok)
    blk_e = jnp.minimum(jnp.searchsorted(pend, jnp.arange(nb, dtype=I32) * tm, side='right'), n_exp - 1).astype(I32)
    n_used = (pend[-1] // tm).astype(I32).reshape(1)
    return row_tok, row_dst, blk_e, n_used


def _combine_kernel(final, h_ref, g_ref, w_ref, s0_ref, s1_ref, s2_ref, s3_ref, *rest):
    w = w_ref[...]
    acc = (w[:, 0:1] * s0_ref[...] + w[:, 1:2] * s1_ref[...]
           + w[:, 2:3] * s2_ref[...] + w[:, 3:4] * s3_ref[...])
    hn = h_ref[...] + g_ref[...] * acc
    if final:
        nf_ref, o_ref, y_ref = rest
        y_ref[...] = hn * lax.rsqrt(jnp.mean(hn * hn, -1, keepdims=True) + EPS) * nf_ref[...]
    else:
        (o_ref,) = rest
    o_ref[...] = hn


def moe_combine(h, gate_rows, gate_w, slots, n_prompt_tiles, norm_f=None):
    t, d = h.shape
    tm = ROW_TILE
    nt = t // tm
    final = norm_f is not None
    in_specs = [pl.BlockSpec((tm, d), lambda i: (i, 0)),
                pl.BlockSpec((tm, d), _mod_map(n_prompt_tiles)),
                pl.BlockSpec((tm, LANES), lambda i: (i, 0))]
    in_specs += [pl.BlockSpec((tm, d), functools.partial(lambda k, i: (k * nt + i, 0), k)) for k in range(TOP_K)]
    args = [h, gate_rows, gate_w, slots, slots, slots, slots]
    out_specs = [pl.BlockSpec((tm, d), lambda i: (i, 0))]
    out_shape = [jax.ShapeDtypeStruct((t, d), F32)]
    if final:
        in_specs.append(pl.BlockSpec((1, d), lambda i: (0, 0)))
        args.append(norm_f.reshape(1, d))
        out_specs.append(pl.BlockSpec((tm, d), lambda i: (i, 0)))
        out_shape.append(jax.ShapeDtypeStruct((t, d), F32))
    return pl.pallas_call(
        functools.partial(_combine_kernel, final),
        grid=(nt,),
        in_specs=in_specs,
        out_specs=out_specs,
        out_shape=out_shape,
        compiler_params=_cparams(("parallel",)),
        name="moe_combine",
    )(*args)


def moe_layer(h, gamma, sh, sc, g2, router_w, router_b, wg, wu, bg, bu, wd, bd, n_prompt_tiles, norm_f=None):
    t = h.shape[0]
    u, top_e, gate_w = moe_route(h, gamma, sh, sc, router_w, router_b, n_prompt_tiles)
    row_tok, row_dst, blk_e, n_used = route_tables(top_e[:, :TOP_K], router_w.shape[1])
    slots = moe_experts(u, row_tok, row_dst, blk_e, n_used, wg, wu, bg, bu, wd, bd, TOP_K * t + 2 * MOE_TILE)
    return moe_combine(h, g2, gate_w, slots, n_prompt_tiles, norm_f)


def _compress_kernel(npg, pt_ref, *refs):
    pages = refs[:npg]
    pe_ref, w1_ref, w2_ref, out_ref, a_ref, xs_ref = refs[npg:]
    s = pl.program_id(1)
    page = pages[0].shape[0]
    m = npg * page // CMP_STRIDE
    half = NSA_KV_HEADS * NSA_HD
    nlb = half // LANES

    @pl.when(s == 0)
    def _init():
        a_ref[0:8, :] = jnp.zeros((8, 2 * half), F32)

    for r, pg in enumerate(pages):
        for cb in range(2 * nlb):
            xs_ref[cb, r * page:(r + 1) * page, :] = pg[:, cb * LANES:(cb + 1) * LANES]

    bms = []
    for c in range(2):
        acc_a = jnp.zeros((m, half), F32)
        acc_b = jnp.zeros((m, half), F32)
        for j in range(CMP_STRIDE):
            rows = jnp.concatenate(
                [xs_ref[c * nlb + cb, pl.ds(j, m, stride=CMP_STRIDE), :] for cb in range(nlb)], axis=1)
            acc_a = acc_a + _mm(rows + pe_ref[c, j:j + 1, :], w1_ref[c, j])
            acc_b = acc_b + _mm(rows + pe_ref[c, CMP_STRIDE + j:CMP_STRIDE + j + 1, :], w1_ref[c, CMP_STRIDE + j])
        a_ref[8:8 + m, c * half:(c + 1) * half] = acc_a
        bms.append(acc_b)
    for c in range(2):
        pre = a_ref[7:7 + m, c * half:(c + 1) * half] + bms[c]
        out_ref[:, c * half:(c + 1) * half] = _mm(_silu(pre), w2_ref[c])
    a_ref[0:8, :] = a_ref[m:m + 8, :]


def compress(pages, page_table, pe_t, w1_bd, w2_bd):
    b, n_pages = page_table.shape
    npg = CMP_PAGES
    page, width = pages.shape[1:]
    half = width // 2
    m = npg * page // CMP_STRIDE
    page_specs = [pl.BlockSpec((None, page, width), functools.partial(lambda r, bi, s, pt: (pt[bi, s * npg + r], 0, 0), r))
                  for r in range(npg)]
    grid_spec = pltpu.PrefetchScalarGridSpec(
        num_scalar_prefetch=1,
        grid=(b, n_pages // npg),
        in_specs=page_specs + [pl.BlockSpec((2, CMP_BLOCK, half), lambda bi, s, pt: (0, 0, 0)),
                               pl.BlockSpec((2, CMP_BLOCK, half, half), lambda bi, s, pt: (0, 0, 0, 0)),
                               pl.BlockSpec((2, half, half), lambda bi, s, pt: (0, 0, 0))],
        out_specs=pl.BlockSpec((None, m, width), lambda bi, s, pt: (bi, s, 0)),
        scratch_shapes=[pltpu.VMEM((m + 8, width), F32),
                        pltpu.VMEM((width // LANES, npg * page, LANES), F32)],
    )
    return pl.pallas_call(
        functools.partial(_compress_kernel, npg),
        grid_spec=grid_spec,
        out_shape=jax.ShapeDtypeStruct((b, n_pages * page // CMP_STRIDE, width), F32),
        compiler_params=_cparams(("parallel", "arbitrary")),
        name="compress",
    )(page_table, *([pages] * npg), pe_t, w1_bd, w2_bd)


def _block_diag4(w):
    eye = jnp.eye(NSA_KV_HEADS, dtype=w.dtype)
    out = jnp.einsum('ij,...ab->...iajb', eye, w)
    return out.reshape(w.shape[:-2] + (NSA_KV_HEADS * w.shape[-2], NSA_KV_HEADS * w.shape[-1]))


def _cmp_softmax(s_t, slope_row, tpos_row):
    ncp = s_t.shape[0]
    blk = _iota((ncp, 1), 0) - 1
    c_end = blk * CMP_STRIDE + (CMP_BLOCK - 1)
    c_ctr = (blk * CMP_STRIDE).astype(F32) + 0.5 * (CMP_BLOCK - 1)
    ok = (c_end <= tpos_row) & (blk >= 0)
    s = s_t - slope_row * (tpos_row.astype(F32) - c_ctr)
    s = jnp.where(ok, s, NEG)
    mx = jnp.max(s, 0, keepdims=True)
    p = jnp.where(ok, jnp.exp(s - mx), 0.0)
    den = jnp.sum(p, 0, keepdims=True)
    return p / jnp.where(den > 0.0, den, 1.0)


def _select_blocks(imp, tpos_row, n_top):
    sj = _iota(imp.shape, 0)
    cur = tpos_row // SEL_BLOCK
    forced = (sj == 0) | (sj == cur) | (sj == cur - 1)
    score = jnp.where(sj <= cur, jnp.where(forced, SEL_FORCE, imp), -1.0)
    ns = imp.shape[0]
    sel = jnp.zeros(imp.shape, F32)
    for _ in range(n_top):
        mx = jnp.max(score, 0, keepdims=True)
        idx = jnp.min(jnp.where(score == mx, sj, ns), 0, keepdims=True)
        pick = sj == idx
        sel = jnp.where(pick & (mx >= 0.0), 1.0, sel)
        score = jnp.where(pick, -2.0, score)
    return sel


def _expand_rows(mask_rows, rep):
    r = mask_rows.shape[0]
    e = (_iota((r * rep, r), 0) // rep == _iota((r * rep, r), 1)).astype(BF16)
    return jnp.dot(e, mask_rows.astype(BF16), preferred_element_type=F32)


def _online_update(s, v_op, m_ref, l_ref, acc_ref, pv):
    m_old = m_ref[...]
    m_new = jnp.maximum(m_old, jnp.max(s, 0, keepdims=True))
    alpha = jnp.exp(m_old - m_new)
    p = jnp.exp(s - m_new)
    l_ref[...] = alpha * l_ref[...] + jnp.sum(p, 0, keepdims=True)
    m_ref[...] = m_new
    return alpha, p


def _nsa_cmp_p_kernel(n_top, qt_ref, kc_ref, vct_ref, selmap_ref, slope_ref, oc_ref, mask_ref):
    i = pl.program_id(1)
    qt = qt_ref[...]
    lanes = qt.shape[1]
    tpos = i * Q_TILE + _iota((1, lanes), 1) % Q_TILE
    p = _cmp_softmax(jnp.dot(kc_ref[...], qt, preferred_element_type=F32), slope_ref[...], tpos)
    oc_ref[...] = _mm(vct_ref[...], p)
    imp_g = _mm_split(selmap_ref[...], p)
    imp = imp_g[:, 0:Q_TILE]
    for g in range(1, NSA_GROUP):
        imp = imp + imp_g[:, g * Q_TILE:(g + 1) * Q_TILE]
    mask_ref[...] = _select_blocks(imp, tpos[:, 0:Q_TILE], n_top)


def nsa_cmp_prompt(qt, kc, vct, selmap_t, slopes, t):
    nkv, nq, hd, lanes = qt.shape
    ncp = kc.shape[1]
    nsp = selmap_t.shape[0]
    n_top = min(SEL_TOP, -(-t // SEL_BLOCK))
    return pl.pallas_call(
        functools.partial(_nsa_cmp_p_kernel, n_top),
        grid=(nkv, nq),
        in_specs=[pl.BlockSpec((None, None, hd, lanes), lambda k, i: (k, i, 0, 0)),
                  pl.BlockSpec((None, ncp, hd), lambda k, i: (k, 0, 0)),
                  pl.BlockSpec((None, hd, ncp), lambda k, i: (k, 0, 0)),
                  pl.BlockSpec((nsp, ncp), lambda k, i: (0, 0)),
                  pl.BlockSpec((None, 1, lanes), lambda k, i: (k, 0, 0))],
        out_specs=[pl.BlockSpec((None, None, hd, lanes), lambda k, i: (k, i, 0, 0)),
                   pl.BlockSpec((None, nsp, Q_TILE), lambda k, i: (k, 0, i))],
        out_shape=[jax.ShapeDtypeStruct((nkv, nq, hd, lanes), F32),
                   jax.ShapeDtypeStruct((nkv, nsp, t), F32)],
        compiler_params=_cparams(("parallel", "parallel")),
        name="nsa_cmp_prompt",
    )(qt, kc, vct, selmap_t, slopes)


def _nsa_slc_p_kernel(qt_ref, ks_ref, vst_ref, kw_ref, vwt_ref, mask_ref, oc_ref, gl_ref, slope_ref,
                      o_ref, m_ref, l_ref, acc_ref):
    i = pl.program_id(1)
    qt = qt_ref[...]
    lanes = qt.shape[1]
    t0 = i * Q_TILE
    tq = t0 + _iota((1, Q_TILE), 1)
    slope = slope_ref[...]
    bpt = KV_TILE // SEL_BLOCK

    def branch(k_ref, vt_ref, j_lo, j_hi, allowed_fn):
        m_ref[...] = jnp.full(m_ref.shape, NEG, F32)
        l_ref[...] = jnp.zeros(l_ref.shape, F32)
        acc_ref[...] = jnp.zeros(acc_ref.shape, F32)

        def body(j, carry):
            k0 = pl.multiple_of(j * KV_TILE, KV_TILE)
            s_t = jnp.dot(k_ref[pl.ds(k0, KV_TILE), :], qt, preferred_element_type=F32)
            spos = k0 + _iota((KV_TILE, 1), 0)
            allowed = allowed_fn(j, spos)
            dist = (tq - spos).astype(F32)
            parts = []
            for g in range(NSA_GROUP):
                sl = slice(g * Q_TILE, (g + 1) * Q_TILE)
                parts.append(jnp.where(allowed, s_t[:, sl] - slope[:, sl] * dist, NEG))
            s = jnp.concatenate(parts, axis=1)
            alpha, p = _online_update(s, None, m_ref, l_ref, acc_ref, None)
            acc_ref[...] = alpha * acc_ref[...] + jnp.dot(
                vt_ref[:, pl.ds(k0, KV_TILE)], p.astype(BF16), preferred_element_type=F32)
            return carry

        lax.fori_loop(j_lo, j_hi, body, 0)
        return acc_ref[...] / l_ref[...]

    def slc_allowed(j, spos):
        rows = mask_ref[pl.ds(pl.multiple_of(j * bpt, bpt), bpt), :]
        return (_expand_rows(rows, SEL_BLOCK) > 0.5) & (spos <= tq)

    def win_allowed(j, spos):
        return (spos <= tq) & (spos > tq - WINDOW)

    n_slc = (t0 + Q_TILE + KV_TILE - 1) // KV_TILE
    o_s = branch(ks_ref, vst_ref, 0, n_slc, slc_allowed)
    j_hi = t0 // KV_TILE + 1
    j_lo = jnp.maximum(j_hi - 1 - WINDOW // KV_TILE, 0)
    o_w = branch(kw_ref, vwt_ref, j_lo, j_hi, win_allowed)
    gates = jax.nn.sigmoid(gl_ref[...])
    o_ref[...] = gates[0:1, :] * oc_ref[...] + gates[1:2, :] * o_s + gates[2:3, :] * o_w


def nsa_slc_prompt(qt, ks, vst, kw, vwt, mask_t, oc_t, gl_t, slopes):
    nkv, nq, hd, lanes = qt.shape
    t = ks.shape[1]
    nsp = mask_t.shape[1]
    tile = lambda: pl.BlockSpec((None, None, hd, lanes), lambda k, i: (k, i, 0, 0))
    return pl.pallas_call(
        _nsa_slc_p_kernel,
        grid=(nkv, nq),
        in_specs=[tile(),
                  pl.BlockSpec((None, t, hd), lambda k, i: (k, 0, 0)),
                  pl.BlockSpec((None, hd, t), lambda k, i: (k, 0, 0)),
                  pl.BlockSpec((None, t, hd), lambda k, i: (k, 0, 0)),
                  pl.BlockSpec((None, hd, t), lambda k, i: (k, 0, 0)),
                  pl.BlockSpec((None, nsp, Q_TILE), lambda k, i: (k, 0, i)),
                  tile(),
                  pl.BlockSpec((None, None, 3, lanes), lambda k, i: (k, i, 0, 0)),
                  pl.BlockSpec((None, 1, lanes), lambda k, i: (k, 0, 0))],
        out_specs=tile(),
        out_shape=jax.ShapeDtypeStruct((nkv, nq, hd, lanes), F32),
        scratch_shapes=[pltpu.VMEM((1, lanes), F32),
                        pltpu.VMEM((1, lanes), F32),
                        pltpu.VMEM((hd, lanes), F32)],
        compiler_params=_cparams(("parallel", "arbitrary")),
        name="nsa_slc_prompt",
    )(qt, ks, vst, kw, vwt, mask_t, oc_t, gl_t, slopes)


def _diag_blocks(o_full, rows_per_head):
    return jnp.concatenate(
        [o_full[k * rows_per_head:(k + 1) * rows_per_head, k * NSA_HD:(k + 1) * NSA_HD]
         for k in range(NSA_KV_HEADS)], axis=0)


def _nsa_cmp_s_kernel(n_top, qbd_ref, kc_ref, vc_ref, selmap_ref, gsum_ref, slope_ref, tpos_ref,
                      oc_ref, mask_ref):
    tpos = tpos_ref[...]
    p = _cmp_softmax(_mm(kc_ref[...], qbd_ref[...]), slope_ref[...], tpos)
    lanes = p.shape[1]
    oc_ref[...] = _diag_blocks(_mm_tn(p, vc_ref[...]), lanes // NSA_KV_HEADS)
    imp = _mm_split_l(_mm_split(selmap_ref[...], p), gsum_ref[...])
    mask_ref[...] = _select_blocks(imp, tpos, n_top)


def nsa_cmp_sample(qbd, cmp_kv, selmap_t, gsum, slope_row, tpos_row, ls):
    b, width, lanes = qbd.shape
    ncp = cmp_kv.shape[1]
    nsp = selmap_t.shape[0]
    n_top = min(SEL_TOP, -(-ls // SEL_BLOCK))
    return pl.pallas_call(
        functools.partial(_nsa_cmp_s_kernel, n_top),
        grid=(b,),
        in_specs=[pl.BlockSpec((None, width, lanes), lambda bi: (bi, 0, 0)),
                  pl.BlockSpec((None, ncp, width), lambda bi: (bi, 0, 0)),
                  pl.BlockSpec((None, ncp, width), lambda bi: (bi, 0, 1)),
                  pl.BlockSpec((nsp, ncp), lambda bi: (0, 0)),
                  pl.BlockSpec((lanes, lanes), lambda bi: (0, 0)),
                  pl.BlockSpec((1, lanes), lambda bi: (0, 0)),
                  pl.BlockSpec((1, lanes), lambda bi: (0, 0))],
        out_specs=[pl.BlockSpec((None, lanes, NSA_HD), lambda bi: (bi, 0, 0)),
                   pl.BlockSpec((None, nsp, lanes), lambda bi: (bi, 0, 0))],
        out_shape=[jax.ShapeDtypeStruct((b, lanes, NSA_HD), F32),
                   jax.ShapeDtypeStruct((b, nsp, lanes), F32)],
        compiler_params=_cparams(("parallel",)),
        name="nsa_cmp_sample",
    )(qbd, cmp_kv, cmp_kv, selmap_t, gsum, slope_row, tpos_row)


def _nsa_slc_s_kernel(npg, past, w_start, pt_ref, *refs):
    pages = refs[:npg]
    (qbd_ref, mask_ref, new_ref, win_ref, oc_ref, gl_ref, slope_ref, tpos_ref,
     o_ref, m_ref, l_ref, acc_ref) = refs[npg:]
    s = pl.program_id(1)
    ns = pl.num_programs(1)
    qbd = qbd_ref[...]
    width = qbd.shape[0]
    lanes = qbd.shape[1]
    tpos = tpos_ref[...]
    tf = tpos.astype(F32)
    slope = slope_ref[...]
    page = pages[0].shape[0]
    bpp = page // SEL_BLOCK

    @pl.when(s == 0)
    def _init():
        m_ref[...] = jnp.full(m_ref.shape, NEG, F32)
        l_ref[...] = jnp.zeros(l_ref.shape, F32)
        acc_ref[...] = jnp.zeros(acc_ref.shape, F32)

    def scores(k_rows, pos0):
        n = k_rows.shape[0]
        spos = pos0 + _iota((n, 1), 0)
        return _mm(k_rows, qbd) - slope * (tf - spos.astype(F32)), spos

    def update(sc, v_rows):
        alpha, p = _online_update(sc, None, m_ref, l_ref, acc_ref, None)
        acc_ref[...] = _row_to_col(alpha) * acc_ref[...] + _mm_tn(p, v_rows)

    k_rows = jnp.concatenate([pg[:, 0:width] for pg in pages], axis=0)
    v_rows = jnp.concatenate([pg[:, width:2 * width] for pg in pages], axis=0)
    sc, spos = scores(k_rows, s * (npg * page))
    nblk = npg * bpp
    rows = mask_ref[pl.ds(pl.multiple_of(s * nblk, nblk), nblk), :]
    allowed = (_expand_rows(rows, SEL_BLOCK) > 0.5) & (spos <= tpos)
    update(jnp.where(allowed, sc, NEG), v_rows)

    @pl.when(s == ns - 1)
    def _finish():
        new = new_ref[...]
        sc_n, spos_n = scores(new[:, 0:width], past)
        blk0 = past // SEL_BLOCK
        row = mask_ref[pl.ds(blk0, 8), :][0:1, :]
        ok_n = (row > 0.5) & (spos_n <= tpos)
        update(jnp.where(ok_n, sc_n, NEG), new[:, width:2 * width])
        o_s = acc_ref[...] / _row_to_col(l_ref[...])
        win = win_ref[...]
        sc_w, wpos = scores(win[:, 0:width], w_start)
        ok_w = (wpos >= 0) & (wpos <= tpos) & (wpos > tpos - WINDOW)
        sc_w = jnp.where(ok_w, sc_w, NEG)
        p_w = jnp.exp(sc_w - jnp.max(sc_w, 0, keepdims=True))
        o_w = _mm_tn(p_w, win[:, width:2 * width]) / _row_to_col(jnp.sum(p_w, 0, keepdims=True))
        rph = lanes // NSA_KV_HEADS
        gates = jax.nn.sigmoid(gl_ref[...])
        o_ref[...] = (gates[:, 0:1] * oc_ref[...] + gates[:, 1:2] * _diag_blocks(o_s, rph)
                      + gates[:, 2:3] * _diag_blocks(o_w, rph))


def nsa_slc_sample(pages, page_table, qbd, mask_t, new_rows, win_all, oc, gl, slope_row, tpos_row,
                   past, w_start):
    b, n_pages = page_table.shape
    npg = SLC_PAGES
    page, width2 = pages.shape[1:]
    width, lanes = qbd.shape[1:]
    nsp = mask_t.shape[1]
    n_new = new_rows.shape[1]
    n_win = win_all.shape[1]
    page_specs = [pl.BlockSpec((None, page, width2), functools.partial(lambda r, bi, s, pt: (pt[bi, s * npg + r], 0, 0), r))
                  for r in range(npg)]
    per_b = lambda shape: pl.BlockSpec((None,) + shape, lambda bi, s, pt: (bi, 0, 0))
    const = lambda shape: pl.BlockSpec(shape, lambda bi, s, pt: (0, 0))
    grid_spec = pltpu.PrefetchScalarGridSpec(
        num_scalar_prefetch=1,
        grid=(b, n_pages // npg),
        in_specs=page_specs + [per_b((width, lanes)), per_b((nsp, lanes)), per_b((n_new, width2)),
                               per_b((n_win, width2)), per_b((lanes, NSA_HD)), per_b((lanes, 8)),
                               const((1, lanes)), const((1, lanes))],
        out_specs=per_b((lanes, NSA_HD)),
        scratch_shapes=[pltpu.VMEM((1, lanes), F32),
                        pltpu.VMEM((1, lanes), F32),
                        pltpu.VMEM((lanes, width), F32)],
    )
    return pl.pallas_call(
        functools.partial(_nsa_slc_s_kernel, npg, past, w_start),
        grid_spec=grid_spec,
        out_shape=jax.ShapeDtypeStruct((b, lanes, NSA_HD), F32),
        compiler_params=_cparams(("parallel", "arbitrary")),
        name="nsa_slc_sample",
    )(page_table, *([pages] * npg), qbd, mask_t, new_rows, win_all, oc, gl, slope_row, tpos_row)


def _alibi_slopes():
    return 2.0 ** (-8.0 * (np.arange(NSA_HEADS) + 1) / NSA_HEADS)


def _selmap_t(nc, ns, ns_pad, ncp):
    r = CMP_BLOCK // CMP_STRIDE
    rs = SEL_BLOCK // CMP_STRIDE
    d = np.arange(nc)[:, None] - rs * np.arange(ns)[None, :]
    m = sum(((d + n >= 0) & (d + n < rs)).astype(np.float32) for n in range(r))
    out = np.zeros((ns_pad, ncp), np.float32)
    out[:ns, 1:nc + 1] = m.T
    return jnp.asarray(out, BF16)


def _mod_rows(chunk, n_sample_rep):
    top = jnp.broadcast_to(chunk[0:1], (ROW_TILE, chunk.shape[1]))
    bottom = jnp.repeat(chunk[1:], n_sample_rep, axis=0)
    return jnp.concatenate([top, bottom], 0)


def nsa_prompt(q, gl, kv, cmp_kv, t):
    nkv, grp, hd = NSA_KV_HEADS, NSA_GROUP, NSA_HD
    nq = t // Q_TILE
    lanes = grp * Q_TILE
    half = nkv * hd
    to_t = lambda x: x.reshape(nq, Q_TILE, nkv, grp, -1)
    qt = (jnp.transpose(to_t(q), (2, 0, 4, 3, 1)).reshape(nkv, nq, hd, lanes) * hd ** -0.5).astype(BF16)
    gl_t = jnp.transpose(gl.reshape(nq, Q_TILE, 3, nkv, grp), (3, 0, 2, 4, 1)).reshape(nkv, nq, 3, lanes)
    heads = lambda x: jnp.transpose(x.reshape(t, nkv, hd), (1, 0, 2)).astype(BF16)
    heads_t = lambda x: jnp.transpose(x.reshape(t, nkv, hd), (1, 2, 0)).astype(BF16)
    ks, vst = heads(kv[:, 2 * half:3 * half]), heads_t(kv[:, 3 * half:4 * half])
    kw, vwt = heads(kv[:, 4 * half:5 * half]), heads_t(kv[:, 5 * half:6 * half])
    ncp = cmp_kv.shape[1]
    kc = jnp.transpose(cmp_kv[0, :, 0:half].reshape(ncp, nkv, hd), (1, 0, 2)).astype(BF16)
    vct = jnp.transpose(cmp_kv[0, :, half:2 * half].reshape(ncp, nkv, hd), (1, 2, 0)).astype(BF16)
    ns = -(-t // SEL_BLOCK)
    nsp = -(-ns // SUBLANES) * SUBLANES
    selmap_t = _selmap_t(ncp - 1, ns, nsp, ncp)
    slopes = jnp.asarray(np.repeat(_alibi_slopes().reshape(nkv, 1, grp, 1), Q_TILE, axis=3).reshape(nkv, 1, lanes), F32)
    oc_t, mask_t = nsa_cmp_prompt(qt, kc, vct, selmap_t, slopes, t)
    o_t = nsa_slc_prompt(qt, ks, vst, kw, vwt, mask_t, oc_t, gl_t, slopes)
    return jnp.transpose(o_t.reshape(nkv, nq, hd, grp, Q_TILE), (1, 4, 0, 3, 2)).reshape(t, nkv * grp * hd)


def nsa_sample(q, gl, kv_new, cmp_kv, slc_pages, page_table, win_all, b, l, past, w_start):
    nkv, grp, hd = NSA_KV_HEADS, NSA_GROUP, NSA_HD
    lanes = nkv * grp * l
    half = nkv * hd
    q5 = q.reshape(b, l, nkv, grp, hd) * hd ** -0.5
    qbd = jnp.einsum('btkgd,kj->bjdkgt', q5, jnp.eye(nkv, dtype=F32)).reshape(b, half, lanes).astype(BF16)
    gl_r = jnp.transpose(gl.reshape(b, l, 3, nkv, grp), (0, 3, 4, 1, 2)).reshape(b, lanes, 3)
    gl_r = jnp.pad(gl_r, ((0, 0), (0, 0), (0, 5)))
    slope_row = jnp.asarray(np.repeat(_alibi_slopes(), l).reshape(1, lanes), F32)
    tpos_row = jnp.asarray(np.tile(past + np.arange(l), nkv * grp).reshape(1, lanes), I32)
    ls = past + l
    ncp = cmp_kv.shape[1]
    ns = -(-ls // SEL_BLOCK)
    nsp = -(-(ns + SUBLANES) // SUBLANES) * SUBLANES
    selmap_t = _selmap_t(ncp - 1, ns, nsp, ncp)
    lane = np.arange(lanes)
    same = (lane[:, None] // (grp * l) == lane[None, :] // (grp * l)) & (lane[:, None] % l == lane[None, :] % l)
    gsum = jnp.asarray(same.astype(np.float32), BF16)
    oc, mask_t = nsa_cmp_sample(qbd, cmp_kv, selmap_t, gsum, slope_row, tpos_row, ls)
    new_rows = kv_new[:, 2 * half:4 * half].reshape(b, l, 2 * half)
    o = nsa_slc_sample(slc_pages, page_table, qbd, mask_t, new_rows, win_all, oc, gl_r, slope_row, tpos_row,
                       past, w_start)
    return jnp.transpose(o.reshape(b, nkv, grp, l, hd), (0, 3, 1, 2, 4)).reshape(b * l, nkv * grp * hd)


def kernel(x_prompt, x_sample, state_gdn, state_conv, cache_cmp_kv, cache_slc_kv, cache_win_kv, page_table,
           c_prompt, c_sample, ada_w, ada_b, norm_mix, norm_ffn, gdn_w_in, gdn_conv_w, gdn_a_log, gdn_dt_bias,
           gdn_norm, gdn_w_out, kv_ada_w, kv_ada_b, kv_norm, kv_w, cmp_pe, cmp_w1, cmp_w2, nsa_w_in, nsa_w_out,
           router_w, router_b, moe_w_gu, moe_b_gu, moe_w_dn, moe_b_dn, norm_f):
    bp, seq, d = x_prompt.shape
    db, dl, _ = x_sample.shape
    assert bp == 1 and db * dl == ROW_TILE and seq % ROW_TILE == 0 and ada_w.shape[0] == 2
    n_pt = seq // ROW_TILE
    page = cache_cmp_kv.shape[1]
    past = page_table.shape[1] * page
    w_buf = cache_win_kv.shape[1]
    w_start = past - w_buf
    nkv, hd = NSA_KV_HEADS, NSA_HD
    half = nkv * hd
    kvw = 2 * half
    hw = GDN_HEADS * GDN_DK
    qkvw = 3 * hw

    c_all = jnp.concatenate([c_prompt, c_sample], 0)
    n_c = c_all.shape[0]
    c_pad = jnp.pad(c_all, ((0, -n_c % SUBLANES), (0, 0)))
    rows6 = lambda mod: [_mod_rows(ch, dl) for ch in jnp.split(mod[:n_c], mod.shape[1] // d, -1)]
    mod0 = rows6(cond_matmul(c_pad, ada_w[0], ada_b[0]))
    mod1 = rows6(cond_matmul(c_pad, ada_w[1], ada_b[1]))
    kv_sh, kv_sc = rows6(cond_matmul(c_pad, kv_ada_w, kv_ada_b))

    h = jnp.concatenate([x_prompt.reshape(seq, d), x_sample.reshape(db * dl, d)], 0)

    sh1, sc1, g1, sh2, sc2, g2 = mod0
    w_in = gdn_w_in[0]
    w_in = jnp.pad(w_in, ((0, 0), (0, -w_in.shape[1] % LANES))).astype(BF16)
    (proj,) = norm_proj(h, [(norm_mix[0], sh1, sc1, w_in)], n_pt)
    zeros_s = jnp.zeros((1,) + state_gdn.shape[2:], F32)
    zeros_c = jnp.zeros((1, 8, qkvw), F32)
    conv_s = jnp.pad(state_conv[0], ((0, 0), (8 - (GDN_CONV - 1), 0), (0, 0)))
    gdn_args = (gdn_conv_w[0], gdn_a_log[0], gdn_dt_bias[0], gdn_norm[0])
    o_p, p_gdn, p_conv = gdn(proj, 0, 1, seq, min(GDN_CHUNK, seq), zeros_s, zeros_c, *gdn_args)
    o_s, s_gdn, s_conv = gdn(proj, seq, db, dl, min(GDN_CHUNK, dl), state_gdn[0], conv_s, *gdn_args)
    h = proj_residual(jnp.concatenate([o_p, o_s], 0), gdn_w_out[0].astype(BF16), h, g1, n_pt)

    def expert_weights(layer):
        w_gu = moe_w_gu[layer]
        return (w_gu[:, :, 0::2].astype(BF16), w_gu[:, :, 1::2].astype(BF16),
                moe_b_gu[layer][:, None, 0::2], moe_b_gu[layer][:, None, 1::2],
                moe_w_dn[layer].astype(BF16), moe_b_dn[layer][:, None, :])

    h = moe_layer(h, norm_ffn[0], sh2, sc2, g2, router_w[0], router_b[0], *expert_weights(0), n_pt)[0]

    sh1, sc1, g1, sh2, sc2, g2 = mod1
    w_q = nsa_w_in[0]
    w_q = jnp.pad(w_q, ((0, 0), (0, -w_q.shape[1] % LANES))).astype(BF16)
    kvp, qp = norm_proj(h, [(kv_norm, kv_sh, kv_sc, kv_w.astype(BF16)), (norm_mix[1], sh1, sc1, w_q)], n_pt)
    nq_cols = NSA_HEADS * hd
    q_all, gl_all = qp[:, :nq_cols], qp[:, nq_cols:nq_cols + 3 * NSA_HEADS]
    kv_p, kv_s = kvp[:seq], kvp[seq:]
    p_cmp, p_slc, p_win = kv_p[:, 0:kvw], kv_p[:, kvw:2 * kvw], kv_p[:, 2 * kvw:3 * kvw]
    s_cmp, s_slc, s_win_new = kv_s[:, 0:kvw], kv_s[:, kvw:2 * kvw], kv_s[:, 2 * kvw:3 * kvw]

    pe_t = jnp.tile(cmp_pe, (1, 1, nkv))
    w1_bd = _block_diag4(cmp_w1).astype(BF16)
    w2_bd = _block_diag4(cmp_w2).astype(BF16)
    cmp_p = compress(p_cmp.reshape(seq // page, page, kvw), jnp.arange(seq // page, dtype=I32).reshape(1, -1),
                     pe_t, w1_bd, w2_bd)
    cmp_s = compress(cache_cmp_kv.reshape(-1, page, kvw), page_table, pe_t, w1_bd, w2_bd)

    win_all = jnp.concatenate([cache_win_kv.reshape(db, w_buf, kvw), s_win_new.reshape(db, dl, kvw)], 1)
    o_p = nsa_prompt(q_all[:seq], gl_all[:seq], kv_p, cmp_p, seq)
    o_s = nsa_sample(q_all[seq:], gl_all[seq:], kv_s, cmp_s, cache_slc_kv.reshape(-1, page, kvw), page_table,
                     win_all, db, dl, past, w_start)
    h = proj_residual(jnp.concatenate([o_p, o_s], 0), nsa_w_out[0].astype(BF16), h, g1, n_pt)
    h, y = moe_layer(h, norm_ffn[1], sh2, sc2, g2, router_w[1], router_b[1], *expert_weights(1), n_pt, norm_f)

    kv5 = lambda x, b_: x.reshape(b_, -1, 2, nkv, hd)
    win_keep = lambda x: x[:, max(0, x.shape[1] - WINDOW):]
    return (y[:seq].reshape(bp, seq, d), y[seq:].reshape(db, dl, d),
            p_gdn[None], p_conv[None],
            kv5(p_cmp, bp), kv5(p_slc, bp), win_keep(kv5(p_win, bp)),
            s_gdn[None], s_conv[None],
            kv5(s_cmp, db), kv5(s_slc, db), win_keep(kv5(win_all, db)))
```

```python
import functools

import jax
import jax.numpy as jnp
import numpy as np
from jax import lax
from jax.experimental import pallas as pl
from jax.experimental.pallas import tpu as pltpu

F32 = jnp.float32
BF16 = jnp.bfloat16
I32 = jnp.int32

GDN_HEADS = 8
GDN_DK = 128
GDN_DV = 128
GDN_CONV = 4
GDN_CHUNK = 64
NSA_HEADS = 16
NSA_KV_HEADS = 4
NSA_GROUP = NSA_HEADS // NSA_KV_HEADS
NSA_HD = 64
CMP_BLOCK = 32
CMP_STRIDE = 16
SEL_BLOCK = 64
SEL_TOP = 16
SEL_FORCE = 1000.0
WINDOW = 512
TOP_K = 4
SWIGLU_LIMIT = 7.0
SWIGLU_ALPHA = 1.702
EPS = 1e-6
NEG = -1e30

LANES = 128
SUBLANES = 8
ROW_TILE = 256
MOE_TILE = 256
KV_TILE = 512
Q_TILE = 128
CMP_PAGES = 16
SLC_PAGES = 8
VMEM_LIMIT = 56 * 1024 * 1024
HIGHEST = lax.Precision.HIGHEST


def _cparams(sem):
    return pltpu.CompilerParams(dimension_semantics=sem, vmem_limit_bytes=VMEM_LIMIT)


def _mm(a, b):
    return jnp.dot(a.astype(BF16), b.astype(BF16), preferred_element_type=F32)


def _mm_nt(a, b):
    return lax.dot_general(a.astype(BF16), b.astype(BF16), (((1,), (1,)), ((), ())),
                           preferred_element_type=F32)


def _mm_tn(a, b):
    return lax.dot_general(a.astype(BF16), b.astype(BF16), (((0,), (0,)), ((), ())),
                           preferred_element_type=F32)


def _mm32(a, b):
    return jnp.dot(a, b, precision=HIGHEST, preferred_element_type=F32)


def _mm_split(a_exact_bf16, p):
    p_hi = p.astype(BF16)
    p_lo = (p - p_hi.astype(F32)).astype(BF16)
    return (jnp.dot(a_exact_bf16, p_hi, preferred_element_type=F32)
            + jnp.dot(a_exact_bf16, p_lo, preferred_element_type=F32))


def _mm_split_l(p, b_exact_bf16):
    p_hi = p.astype(BF16)
    p_lo = (p - p_hi.astype(F32)).astype(BF16)
    return (jnp.dot(p_hi, b_exact_bf16, preferred_element_type=F32)
            + jnp.dot(p_lo, b_exact_bf16, preferred_element_type=F32))


def _row_to_col(row):
    n = row.shape[1]
    eye = _iota((n, n), 0) == _iota((n, n), 1)
    return jnp.sum(jnp.where(eye, jnp.broadcast_to(row, (n, n)), 0.0), axis=1, keepdims=True)


def _silu(x):
    return x * jax.nn.sigmoid(x)


def _iota(shape, dim):
    return lax.broadcasted_iota(I32, shape, dim)


def _cond_kernel(c_ref, w_ref, b_ref, o_ref):
    o_ref[...] = _mm(_silu(c_ref[...]), w_ref[...]) + b_ref[...]


def cond_matmul(c, w, b, tn=1024):
    m, d = c.shape
    n = w.shape[1]
    return pl.pallas_call(
        _cond_kernel,
        grid=(n // tn,),
        in_specs=[pl.BlockSpec((m, d), lambda j: (0, 0)),
                  pl.BlockSpec((d, tn), lambda j: (0, j)),
                  pl.BlockSpec((1, tn), lambda j: (0, j))],
        out_specs=pl.BlockSpec((m, tn), lambda j: (0, j)),
        out_shape=jax.ShapeDtypeStruct((m, n), F32),
        compiler_params=_cparams(("arbitrary",)),
        name="cond_matmul",
    )(c, w, b.reshape(1, n))


def _mod_map(n_prompt_tiles):
    return lambda i: (jnp.where(i < n_prompt_tiles, 0, 1), 0)


def _norm_proj_kernel(n_heads, x_ref, *refs):
    x = x_ref[...]
    xn = x * lax.rsqrt(jnp.mean(x * x, -1, keepdims=True) + EPS)
    for i in range(n_heads):
        g_ref, sh_ref, sc_ref, w_ref = refs[4 * i:4 * i + 4]
        o_ref = refs[4 * n_heads + i]
        u = (xn * g_ref[...]) * (1.0 + sc_ref[...]) + sh_ref[...]
        o_ref[...] = _mm(u, w_ref[...])


def norm_proj(h, heads, n_prompt_tiles):
    t, d = h.shape
    tm = ROW_TILE
    in_specs = [pl.BlockSpec((tm, d), lambda i: (i, 0))]
    args = [h]
    out_specs, out_shapes = [], []
    for gamma, sh, sc, w in heads:
        n = w.shape[1]
        in_specs += [pl.BlockSpec((1, d), lambda i: (0, 0)),
                     pl.BlockSpec((tm, d), _mod_map(n_prompt_tiles)),
                     pl.BlockSpec((tm, d), _mod_map(n_prompt_tiles)),
                     pl.BlockSpec((d, n), lambda i: (0, 0))]
        args += [gamma.reshape(1, d), sh, sc, w]
        out_specs.append(pl.BlockSpec((tm, n), lambda i: (i, 0)))
        out_shapes.append(jax.ShapeDtypeStruct((t, n), F32))
    return pl.pallas_call(
        functools.partial(_norm_proj_kernel, len(heads)),
        grid=(t // tm,),
        in_specs=in_specs,
        out_specs=out_specs,
        out_shape=out_shapes,
        compiler_params=_cparams(("parallel",)),
        name="norm_proj",
    )(*args)


def _proj_res_kernel(a_ref, w_ref, h_ref, g_ref, o_ref):
    o_ref[...] = h_ref[...] + g_ref[...] * _mm(a_ref[...], w_ref[...])


def proj_residual(a, w, h, gate_rows, n_prompt_tiles):
    t, k = a.shape
    d = w.shape[1]
    tm = ROW_TILE
    return pl.pallas_call(
        _proj_res_kernel,
        grid=(t // tm,),
        in_specs=[pl.BlockSpec((tm, k), lambda i: (i, 0)),
                  pl.BlockSpec((k, d), lambda i: (0, 0)),
                  pl.BlockSpec((tm, d), lambda i: (i, 0)),
                  pl.BlockSpec((tm, d), _mod_map(n_prompt_tiles))],
        out_specs=pl.BlockSpec((tm, d), lambda i: (i, 0)),
        out_shape=jax.ShapeDtypeStruct((t, d), F32),
        compiler_params=_cparams(("parallel",)),
        name="proj_residual",
    )(a, w, h, gate_rows)


def _mm3(a, b):
    a_hi = a.astype(BF16)
    b_hi = b.astype(BF16)
    a_lo = (a - a_hi.astype(F32)).astype(BF16)
    b_lo = (b - b_hi.astype(F32)).astype(BF16)
    dot = lambda x, y: jnp.dot(x, y, preferred_element_type=F32)
    return dot(a_hi, b_hi) + (dot(a_hi, b_lo) + dot(a_lo, b_hi))


def _tri_inv(ms, c):
    r = _iota((c, c), 0)
    col = _iota((c, c), 1)
    eye = (r == col).astype(F32)
    same8 = (r // 8) == (col // 8)
    ds = [jnp.where(same8, m, 0.0) for m in ms]
    d2s = [_mm3(d, d) for d in ds]
    xs = [eye - d for d in ds]
    d4s = [_mm3(d2, d2) for d2 in d2s]
    xs = [x + _mm3(x, d2) for x, d2 in zip(xs, d2s)]
    xs = [x + _mm3(x, d4) for x, d4 in zip(xs, d4s)]
    size = 8
    while size < c:
        off = ((r // (2 * size)) == (col // (2 * size))) & ((r // size) != (col // size))
        ys = [_mm3(x, jnp.where(off, m, 0.0)) for x, m in zip(xs, ms)]
        xs = [x - _mm3(y, x) for x, y in zip(xs, ys)]
        size *= 2
    return xs


def _cumsum_rows(x):
    r = _iota(x.shape, 0)
    sh = 1
    while sh < x.shape[0]:
        x = x + jnp.where(r >= sh, pltpu.roll(x, sh, 0), 0.0)
        sh *= 2
    return x


def _gdn_kernel(c, q_ref, k_ref, v_ref, z_ref, ba_ref, conv0_ref, s0_ref, cw_ref, alog_ref, dtb_ref, ng_ref,
                o_ref, sfin_ref, convn_ref, xs_ref, xc_ref, st_ref):
    n = pl.program_id(1)
    nh, dk = GDN_HEADS, GDN_DK
    hw = nh * dk

    @pl.when(n == 0)
    def _init():
        xs_ref[0:8, :] = conv0_ref[...]
        st_ref[...] = s0_ref[...]

    xs_ref[8:8 + c, 0:hw] = q_ref[...]
    xs_ref[8:8 + c, hw:2 * hw] = k_ref[...]
    xs_ref[8:8 + c, 2 * hw:3 * hw] = v_ref[...]
    acc = cw_ref[3:4, :] * xs_ref[8:8 + c, :]
    for j in range(GDN_CONV - 1):
        acc = acc + cw_ref[j:j + 1, :] * xs_ref[5 + j:5 + j + c, :]
    xc_ref[...] = _silu(acc)
    convn_ref[...] = xs_ref[5 + c:8 + c, :]
    xs_ref[0:8, :] = xs_ref[c:c + 8, :]

    ba = ba_ref[...]
    beta_all = jax.nn.sigmoid(ba)
    xa = ba + dtb_ref[...]
    softplus = jnp.maximum(xa, 0.0) + jnp.log(1.0 + jnp.exp(-jnp.abs(xa)))
    g_all = -jnp.exp(alog_ref[...]) * softplus

    r = _iota((c, c), 0)
    col = _iota((c, c), 1)
    incl = r >= col
    strict = r > col
    eye = r == col
    heads = range(nh)
    qh = [xc_ref[:, h * dk:(h + 1) * dk] for h in heads]
    kh = [xc_ref[:, hw + h * dk:hw + (h + 1) * dk] for h in heads]
    vh = [xc_ref[:, 2 * hw + h * dk:2 * hw + (h + 1) * dk] for h in heads]
    qn = [x * lax.rsqrt(jnp.sum(x * x, -1, keepdims=True) + EPS) * (dk ** -0.5) for x in qh]
    kn = [x * lax.rsqrt(jnp.sum(x * x, -1, keepdims=True) + EPS) for x in kh]
    beta = [beta_all[:, h:h + 1] for h in heads]
    gc_b = [_cumsum_rows(jnp.broadcast_to(g_all[:, nh + h:nh + h + 1], (c, dk))) for h in heads]
    gc_row = [jnp.sum(jnp.where(eye, g[:, 0:c], 0.0), axis=0, keepdims=True) for g in gc_b]
    decay = [jnp.where(incl, jnp.exp(jnp.where(incl, g[:, 0:c] - gr, 0.0)), 0.0) for g, gr in zip(gc_b, gc_row)]
    kb = [k * b for k, b in zip(kn, beta)]
    vb = [v * b for v, b in zip(vh, beta)]
    kk = [_mm_nt(a, k) for a, k in zip(kb, kn)]
    qk = [_mm_nt(q, k) for q, k in zip(qn, kn)]
    t_inv = _tri_inv([jnp.where(strict, x * dcy, 0.0) for x, dcy in zip(kk, decay)], c)
    attn = [jnp.where(incl, x * dcy, 0.0) for x, dcy in zip(qk, decay)]
    egc = [jnp.exp(g) for g in gc_b]
    u = [_mm(t, v) for t, v in zip(t_inv, vb)]
    w = [_mm(t, a * e) for t, a, e in zip(t_inv, kb, egc)]
    gl_b = [g[c - 1:c, :] for g in gc_b]
    kg = [k * jnp.exp(gl - g) for k, gl, g in zip(kn, gl_b, gc_b)]
    s = [st_ref[h] for h in heads]
    ws = [_mm(a, b) for a, b in zip(w, s)]
    o1 = [_mm(q * e, b) for q, e, b in zip(qn, egc, s)]
    v_new = [a - b for a, b in zip(u, ws)]
    o2 = [_mm(a, v) for a, v in zip(attn, v_new)]
    kv = [_mm_tn(k, v) for k, v in zip(kg, v_new)]
    for h in heads:
        s_new = s[h] * jnp.exp(gl_b[h]) + kv[h]
        st_ref[h] = s_new
        sfin_ref[h] = s_new
        o = o1[h] + o2[h]
        on = o * lax.rsqrt(jnp.mean(o * o, -1, keepdims=True) + EPS) * ng_ref[...]
        o_ref[:, h * dk:(h + 1) * dk] = on * _silu(z_ref[:, h * dk:(h + 1) * dk])


def gdn(proj, row_off, b, l, c, s0, conv0, conv_w, a_log, dt_bias, norm_g):
    nh, dk = GDN_HEADS, GDN_DK
    hw = nh * dk
    nchunks = l // c
    rb0 = row_off // c
    row = lambda bi, n: rb0 + bi * nchunks + n
    gate_blk = 4 * hw // LANES
    lane_pad = jnp.zeros((LANES - 2 * nh,), F32)
    alog_row = jnp.concatenate([jnp.zeros((nh,), F32), a_log, lane_pad]).reshape(1, LANES)
    dtb_row = jnp.concatenate([jnp.zeros((nh,), F32), dt_bias, lane_pad]).reshape(1, LANES)
    return pl.pallas_call(
        functools.partial(_gdn_kernel, c),
        grid=(b, nchunks),
        in_specs=[pl.BlockSpec((c, hw), lambda bi, n: (row(bi, n), 0)),
                  pl.BlockSpec((c, hw), lambda bi, n: (row(bi, n), 1)),
                  pl.BlockSpec((c, hw), lambda bi, n: (row(bi, n), 2)),
                  pl.BlockSpec((c, hw), lambda bi, n: (row(bi, n), 3)),
                  pl.BlockSpec((c, LANES), lambda bi, n: (row(bi, n), gate_blk)),
                  pl.BlockSpec((None, 8, 3 * hw), lambda bi, n: (bi, 0, 0)),
                  pl.BlockSpec((None, nh, dk, GDN_DV), lambda bi, n: (bi, 0, 0, 0)),
                  pl.BlockSpec((GDN_CONV, 3 * hw), lambda bi, n: (0, 0)),
                  pl.BlockSpec((1, LANES), lambda bi, n: (0, 0)),
                  pl.BlockSpec((1, LANES), lambda bi, n: (0, 0)),
                  pl.BlockSpec((1, GDN_DV), lambda bi, n: (0, 0))],
        out_specs=[pl.BlockSpec((c, hw), lambda bi, n: (bi * nchunks + n, 0)),
                   pl.BlockSpec((None, nh, dk, GDN_DV), lambda bi, n: (bi, 0, 0, 0)),
                   pl.BlockSpec((None, GDN_CONV - 1, 3 * hw), lambda bi, n: (bi, 0, 0))],
        out_shape=[jax.ShapeDtypeStruct((b * l, hw), F32),
                   jax.ShapeDtypeStruct((b, nh, dk, GDN_DV), F32),
                   jax.ShapeDtypeStruct((b, GDN_CONV - 1, 3 * hw), F32)],
        scratch_shapes=[pltpu.VMEM((c + 8, 3 * hw), F32),
                        pltpu.VMEM((c, 3 * hw), F32),
                        pltpu.VMEM((nh, dk, GDN_DV), F32)],
        compiler_params=_cparams(("parallel", "arbitrary")),
        name="gdn",
    )(proj, proj, proj, proj, proj, conv0, s0, conv_w, alog_row, dtb_row, norm_g.reshape(1, GDN_DV))


def _route_kernel(x_ref, g_ref, sh_ref, sc_ref, rw_ref, rb_ref, u_ref, e_ref, w_ref):
    x = x_ref[...]
    xn = x * lax.rsqrt(jnp.mean(x * x, -1, keepdims=True) + EPS)
    u = (xn * g_ref[...]) * (1.0 + sc_ref[...]) + sh_ref[...]
    u_ref[...] = u
    logits = _mm32(u, rw_ref[...]) + rb_ref[...]
    tm, ne = logits.shape
    eidx = _iota((tm, ne), 1)
    lane = _iota((tm, LANES), 1)
    e_out = jnp.zeros((tm, LANES), I32)
    vals = []
    for k in range(TOP_K):
        m = jnp.max(logits, -1, keepdims=True)
        idx = jnp.min(jnp.where(logits == m, eidx, ne), -1, keepdims=True)
        logits = jnp.where(eidx == idx, -3e38, logits)
        vals.append(m)
        e_out = jnp.where(lane == k, idx, e_out)
    ex = [jnp.exp(v - vals[0]) for v in vals]
    den = ex[0] + ex[1] + ex[2] + ex[3]
    w_out = jnp.zeros((tm, LANES), F32)
    for k in range(TOP_K):
        w_out = jnp.where(lane == k, ex[k] / den, w_out)
    e_ref[...] = e_out
    w_ref[...] = w_out


def moe_route(h, gamma, sh, sc, router_w, router_b, n_prompt_tiles):
    t, d = h.shape
    ne = router_w.shape[1]
    tm = ROW_TILE
    return pl.pallas_call(
        _route_kernel,
        grid=(t // tm,),
        in_specs=[pl.BlockSpec((tm, d), lambda i: (i, 0)),
                  pl.BlockSpec((1, d), lambda i: (0, 0)),
                  pl.BlockSpec((tm, d), _mod_map(n_prompt_tiles)),
                  pl.BlockSpec((tm, d), _mod_map(n_prompt_tiles)),
                  pl.BlockSpec((d, ne), lambda i: (0, 0)),
                  pl.BlockSpec((1, ne), lambda i: (0, 0))],
        out_specs=[pl.BlockSpec((tm, d), lambda i: (i, 0)),
                   pl.BlockSpec((tm, LANES), lambda i: (i, 0)),
                   pl.BlockSpec((tm, LANES), lambda i: (i, 0))],
        out_shape=[jax.ShapeDtypeStruct((t, d), F32),
                   jax.ShapeDtypeStruct((t, LANES), I32),
                   jax.ShapeDtypeStruct((t, LANES), F32)],
        compiler_params=_cparams(("parallel",)),
        name="moe_route",
    )(h, gamma.reshape(1, d), sh, sc, router_w, router_b.reshape(1, ne))


def _expert_kernel(be_ref, nu_ref, tok0_ref, tokn_ref, dst_ref, x_hbm, wgu_ref, bgu_ref, wd_ref, bd_ref,
                   out_hbm, xbuf, ybuf, gsem, ssem):
    i = pl.program_id(0)
    n_used = nu_ref[0]
    slot = lax.rem(i, 2)
    tm = xbuf.shape[1]
    dump0 = out_hbm.shape[0] - 2 * tm

    def row_in(tok, s, j):
        return pltpu.make_async_copy(x_hbm.at[pl.ds(tok, 1)], xbuf.at[s, pl.ds(j, 1)], gsem.at[s])

    def row_out(dst, s, j):
        return pltpu.make_async_copy(ybuf.at[s, pl.ds(j, 1)], out_hbm.at[pl.ds(dst, 1)], ssem.at[s])

    def start_gather(tok_ref, s):
        for j in range(tm):
            row_in(tok_ref[0, j], s, j).start(priority=j % 2)

    def wait_gather(s):
        for j in range(tm):
            row_in(0, s, j).wait()

    def wait_scatter(s):
        for j in range(tm):
            row_out(0, s, j).wait()

    @pl.when(i == 0)
    def _first():
        start_gather(tok0_ref, 0)
        ybuf[...] = jnp.zeros(ybuf.shape, F32)
        for s in range(2):
            pltpu.make_async_copy(ybuf.at[s], out_hbm.at[pl.ds(dump0 + s * tm, tm)], ssem.at[s]).start()

    @pl.when(i < n_used)
    def _block():
        wait_gather(slot)
        start_gather(tokn_ref, 1 - slot)
        wait_scatter(slot)
        gu = _mm(xbuf[slot], wgu_ref[...]) + bgu_ref[...]
        parts = []
        for c in range(gu.shape[1] // LANES):
            pair = gu[:, c * LANES:(c + 1) * LANES]
            g = jnp.minimum(pair, SWIGLU_LIMIT)
            u = pltpu.roll(jnp.clip(pair, -SWIGLU_LIMIT, SWIGLU_LIMIT), LANES - 1, 1)
            parts.append((u + 1.0) * (g * jax.nn.sigmoid(SWIGLU_ALPHA * g)))
        ybuf[slot] = _mm(jnp.concatenate(parts, axis=1), wd_ref[...]) + bd_ref[...]
        for j in range(tm):
            row_out(dst_ref[0, j], slot, j).start(priority=j % 2)

        @pl.when(i == n_used - 1)
        def _drain():
            wait_gather(1 - slot)
            wait_scatter(slot)
            wait_scatter(1 - slot)


def moe_experts(u, row_tok, row_dst, blk_e, n_used, wgu, bgu, wd, bd, n_out_rows):
    t, d = u.shape
    ne, _, f2 = wgu.shape
    tm = MOE_TILE
    nb = row_tok.shape[0] // tm
    tok3 = row_tok.reshape(nb, 1, tm)
    dst3 = row_dst.reshape(nb, 1, tm)
    smem_blk = lambda imap: pl.BlockSpec((None, 1, tm), imap, memory_space=pltpu.SMEM)
    grid_spec = pltpu.PrefetchScalarGridSpec(
        num_scalar_prefetch=2,
        grid=(nb,),
        in_specs=[smem_blk(lambda i, be, nu: (i, 0, 0)),
                  smem_blk(lambda i, be, nu: (jnp.minimum(i + 1, nb - 1), 0, 0)),
                  smem_blk(lambda i, be, nu: (i, 0, 0)),
                  pl.BlockSpec(memory_space=pl.ANY),
                  pl.BlockSpec((None, d, f2), lambda i, be, nu: (be[i], 0, 0)),
                  pl.BlockSpec((None, 1, f2), lambda i, be, nu: (be[i], 0, 0)),
                  pl.BlockSpec((None, f2, d), lambda i, be, nu: (be[i], 0, 0)),
                  pl.BlockSpec((None, 1, d), lambda i, be, nu: (be[i], 0, 0))],
        out_specs=pl.BlockSpec(memory_space=pl.ANY),
        scratch_shapes=[pltpu.VMEM((2, tm, d), F32),
                        pltpu.VMEM((2, tm, d), F32),
                        pltpu.SemaphoreType.DMA((2,)),
                        pltpu.SemaphoreType.DMA((2,))],
    )
    return pl.pallas_call(
        _expert_kernel,
        grid_spec=grid_spec,
        out_shape=jax.ShapeDtypeStruct((n_out_rows, d), F32),
        compiler_params=_cparams(("arbitrary",)),
        name="moe_experts",
    )(blk_e, n_used, tok3, tok3, dst3, u, wgu, bgu, wd, bd)


def route_tables(top_e, n_exp):
    t = top_e.shape[0]
    tm = MOE_TILE
    m = t * TOP_K
    flat_e = top_e.reshape(-1)
    order = jnp.argsort(flat_e, stable=True).astype(I32)
    se = flat_e[order]
    counts = jnp.bincount(flat_e, length=n_exp).astype(I32)
    padded = (counts + tm - 1) // tm * tm
    start = jnp.cumsum(counts) - counts
    pend = jnp.cumsum(padded)
    pstart = pend - padded
    dest = pstart[se] + jnp.arange(m, dtype=I32) - start[se]
    nb = -(-(m + n_exp * (tm - 1)) // tm)
    rows = nb * tm
    tok = order // TOP_K
    slot = order % TOP_K
    ridx = jnp.arange(rows, dtype=I32)
    pad_dst = TOP_K * t + ((ridx // tm) % 2) * tm + ridx % tm
    row_tok = jnp.zeros((rows,), I32).at[dest].set(tok)
    row_dst = pad_dst.at[dest].set(slot * t + tok)
    blk_e = jnp.minimum(jnp.searchsorted(pend, jnp.arange(nb, dtype=I32) * tm, side='right'), n_exp - 1).astype(I32)
    n_used = (pend[-1] // tm).astype(I32).reshape(1)
    return row_tok, row_dst, blk_e, n_used


def _combine_kernel(final, h_ref, g_ref, w_ref, s0_ref, s1_ref, s2_ref, s3_ref, *rest):
    w = w_ref[...]
    acc = (w[:, 0:1] * s0_ref[...] + w[:, 1:2] * s1_ref[...]
           + w[:, 2:3] * s2_ref[...] + w[:, 3:4] * s3_ref[...])
    hn = h_ref[...] + g_ref[...] * acc
    if final:
        nf_ref, o_ref, y_ref = rest
        y_ref[...] = hn * lax.rsqrt(jnp.mean(hn * hn, -1, keepdims=True) + EPS) * nf_ref[...]
    else:
        (o_ref,) = rest
    o_ref[...] = hn


def moe_combine(h, gate_rows, gate_w, slots, n_prompt_tiles, norm_f=None):
    t, d = h.shape
    tm = ROW_TILE
    nt = t // tm
    final = norm_f is not None
    in_specs = [pl.BlockSpec((tm, d), lambda i: (i, 0)),
                pl.BlockSpec((tm, d), _mod_map(n_prompt_tiles)),
                pl.BlockSpec((tm, LANES), lambda i: (i, 0))]
    in_specs += [pl.BlockSpec((tm, d), functools.partial(lambda k, i: (k * nt + i, 0), k)) for k in range(TOP_K)]
    args = [h, gate_rows, gate_w, slots, slots, slots, slots]
    out_specs = [pl.BlockSpec((tm, d), lambda i: (i, 0))]
    out_shape = [jax.ShapeDtypeStruct((t, d), F32)]
    if final:
        in_specs.append(pl.BlockSpec((1, d), lambda i: (0, 0)))
        args.append(norm_f.reshape(1, d))
        out_specs.append(pl.BlockSpec((tm, d), lambda i: (i, 0)))
        out_shape.append(jax.ShapeDtypeStruct((t, d), F32))
    return pl.pallas_call(
        functools.partial(_combine_kernel, final),
        grid=(nt,),
        in_specs=in_specs,
        out_specs=out_specs,
        out_shape=out_shape,
        compiler_params=_cparams(("parallel",)),
        name="moe_combine",
    )(*args)


def moe_layer(h, gamma, sh, sc, g2, router_w, router_b, wgu, bgu, wd, bd, n_prompt_tiles, norm_f=None):
    t = h.shape[0]
    u, top_e, gate_w = moe_route(h, gamma, sh, sc, router_w, router_b, n_prompt_tiles)
    row_tok, row_dst, blk_e, n_used = route_tables(top_e[:, :TOP_K], router_w.shape[1])
    slots = moe_experts(u, row_tok, row_dst, blk_e, n_used, wgu, bgu, wd, bd, TOP_K * t + 2 * MOE_TILE)
    return moe_combine(h, g2, gate_w, slots, n_prompt_tiles, norm_f)


def _compress_kernel(npg, pt_ref, *refs):
    pages = refs[:npg]
    pe_ref, w1_ref, w2_ref, out_ref, a_ref, xs_ref = refs[npg:]
    s = pl.program_id(1)
    page = pages[0].shape[0]
    m = npg * page // CMP_STRIDE
    half = NSA_KV_HEADS * NSA_HD
    nlb = half // LANES

    @pl.when(s == 0)
    def _init():
        a_ref[0:8, :] = jnp.zeros((8, 2 * half), F32)

    for r, pg in enumerate(pages):
        for cb in range(2 * nlb):
            xs_ref[cb, r * page:(r + 1) * page, :] = pg[:, cb * LANES:(cb + 1) * LANES]

    bms = []
    for c in range(2):
        acc_a = jnp.zeros((m, half), F32)
        acc_b = jnp.zeros((m, half), F32)
        for j in range(CMP_STRIDE):
            rows = jnp.concatenate(
                [xs_ref[c * nlb + cb, pl.ds(j, m, stride=CMP_STRIDE), :] for cb in range(nlb)], axis=1)
            acc_a = acc_a + _mm(rows + pe_ref[c, j:j + 1, :], w1_ref[c, j])
            acc_b = acc_b + _mm(rows + pe_ref[c, CMP_STRIDE + j:CMP_STRIDE + j + 1, :], w1_ref[c, CMP_STRIDE + j])
        a_ref[8:8 + m, c * half:(c + 1) * half] = acc_a
        bms.append(acc_b)
    for c in range(2):
        pre = a_ref[7:7 + m, c * half:(c + 1) * half] + bms[c]
        out_ref[:, c * half:(c + 1) * half] = _mm(_silu(pre), w2_ref[c])
    a_ref[0:8, :] = a_ref[m:m + 8, :]


def compress(pages, page_table, pe_t, w1_bd, w2_bd):
    b, n_pages = page_table.shape
    npg = CMP_PAGES
    page, width = pages.shape[1:]
    half = width // 2
    m = npg * page // CMP_STRIDE
    page_specs = [pl.BlockSpec((None, page, width), functools.partial(lambda r, bi, s, pt: (pt[bi, s * npg + r], 0, 0), r))
                  for r in range(npg)]
    grid_spec = pltpu.PrefetchScalarGridSpec(
        num_scalar_prefetch=1,
        grid=(b, n_pages // npg),
        in_specs=page_specs + [pl.BlockSpec((2, CMP_BLOCK, half), lambda bi, s, pt: (0, 0, 0)),
                               pl.BlockSpec((2, CMP_BLOCK, half, half), lambda bi, s, pt: (0, 0, 0, 0)),
                               pl.BlockSpec((2, half, half), lambda bi, s, pt: (0, 0, 0))],
        out_specs=pl.BlockSpec((None, m, width), lambda bi, s, pt: (bi, s, 0)),
        scratch_shapes=[pltpu.VMEM((m + 8, width), F32),
                        pltpu.VMEM((width // LANES, npg * page, LANES), F32)],
    )
    return pl.pallas_call(
        functools.partial(_compress_kernel, npg),
        grid_spec=grid_spec,
        out_shape=jax.ShapeDtypeStruct((b, n_pages * page // CMP_STRIDE, width), F32),
        compiler_params=_cparams(("parallel", "arbitrary")),
        name="compress",
    )(page_table, *([pages] * npg), pe_t, w1_bd, w2_bd)


def _block_diag4(w):
    eye = jnp.eye(NSA_KV_HEADS, dtype=w.dtype)
    out = jnp.einsum('ij,...ab->...iajb', eye, w)
    return out.reshape(w.shape[:-2] + (NSA_KV_HEADS * w.shape[-2], NSA_KV_HEADS * w.shape[-1]))


def _cmp_softmax(s_t, slope_row, tpos_row):
    ncp = s_t.shape[0]
    blk = _iota((ncp, 1), 0) - 1
    c_end = blk * CMP_STRIDE + (CMP_BLOCK - 1)
    c_ctr = (blk * CMP_STRIDE).astype(F32) + 0.5 * (CMP_BLOCK - 1)
    ok = (c_end <= tpos_row) & (blk >= 0)
    s = s_t - slope_row * (tpos_row.astype(F32) - c_ctr)
    s = jnp.where(ok, s, NEG)
    mx = jnp.max(s, 0, keepdims=True)
    p = jnp.where(ok, jnp.exp(s - mx), 0.0)
    den = jnp.sum(p, 0, keepdims=True)
    return p / jnp.where(den > 0.0, den, 1.0)


def _select_blocks(imp, tpos_row, n_top):
    sj = _iota(imp.shape, 0)
    cur = tpos_row // SEL_BLOCK
    forced = (sj == 0) | (sj == cur) | (sj == cur - 1)
    score = jnp.where(sj <= cur, jnp.where(forced, SEL_FORCE, imp), -1.0)
    ns = imp.shape[0]
    sel = jnp.zeros(imp.shape, F32)
    for _ in range(n_top):
        mx = jnp.max(score, 0, keepdims=True)
        idx = jnp.min(jnp.where(score == mx, sj, ns), 0, keepdims=True)
        pick = sj == idx
        sel = jnp.where(pick & (mx >= 0.0), 1.0, sel)
        score = jnp.where(pick, -2.0, score)
    return sel


def _expand_rows(mask_rows, rep):
    r = mask_rows.shape[0]
    e = (_iota((r * rep, r), 0) // rep == _iota((r * rep, r), 1)).astype(BF16)
    return jnp.dot(e, mask_rows.astype(BF16), preferred_element_type=F32)


def _online_update(s, v_op, m_ref, l_ref, acc_ref, pv):
    m_old = m_ref[...]
    m_new = jnp.maximum(m_old, jnp.max(s, 0, keepdims=True))
    alpha = jnp.exp(m_old - m_new)
    p = jnp.exp(s - m_new)
    l_ref[...] = alpha * l_ref[...] + jnp.sum(p, 0, keepdims=True)
    m_ref[...] = m_new
    return alpha, p


def _nsa_cmp_p_kernel(n_top, qt_ref, kc_ref, vct_ref, selmap_ref, slope_ref, oc_ref, mask_ref):
    i = pl.program_id(1)
    qt = qt_ref[...]
    lanes = qt.shape[1]
    tpos = i * Q_TILE + _iota((1, lanes), 1) % Q_TILE
    p = _cmp_softmax(jnp.dot(kc_ref[...], qt, preferred_element_type=F32), slope_ref[...], tpos)
    oc_ref[...] = _mm(vct_ref[...], p)
    imp_g = _mm_split(selmap_ref[...], p)
    imp = imp_g[:, 0:Q_TILE]
    for g in range(1, NSA_GROUP):
        imp = imp + imp_g[:, g * Q_TILE:(g + 1) * Q_TILE]
    mask_ref[...] = _select_blocks(imp, tpos[:, 0:Q_TILE], n_top)


def nsa_cmp_prompt(qt, kc, vct, selmap_t, slopes, t):
    nkv, nq, hd, lanes = qt.shape
    ncp = kc.shape[1]
    nsp = selmap_t.shape[0]
    n_top = min(SEL_TOP, -(-t // SEL_BLOCK))
    return pl.pallas_call(
        functools.partial(_nsa_cmp_p_kernel, n_top),
        grid=(nkv, nq),
        in_specs=[pl.BlockSpec((None, None, hd, lanes), lambda k, i: (k, i, 0, 0)),
                  pl.BlockSpec((None, ncp, hd), lambda k, i: (k, 0, 0)),
                  pl.BlockSpec((None, hd, ncp), lambda k, i: (k, 0, 0)),
                  pl.BlockSpec((nsp, ncp), lambda k, i: (0, 0)),
                  pl.BlockSpec((None, 1, lanes), lambda k, i: (k, 0, 0))],
        out_specs=[pl.BlockSpec((None, None, hd, lanes), lambda k, i: (k, i, 0, 0)),
                   pl.BlockSpec((None, nsp, Q_TILE), lambda k, i: (k, 0, i))],
        out_shape=[jax.ShapeDtypeStruct((nkv, nq, hd, lanes), F32),
                   jax.ShapeDtypeStruct((nkv, nsp, t), F32)],
        compiler_params=_cparams(("parallel", "parallel")),
        name="nsa_cmp_prompt",
    )(qt, kc, vct, selmap_t, slopes)


def _nsa_slc_p_kernel(qt_ref, ks_ref, vst_ref, kw_ref, vwt_ref, mask_ref, oc_ref, gl_ref, slope_ref,
                      o_ref, m_ref, l_ref, acc_ref):
    i = pl.program_id(1)
    qt = qt_ref[...]
    lanes = qt.shape[1]
    t0 = i * Q_TILE
    tq = t0 + _iota((1, Q_TILE), 1)
    slope = slope_ref[...]
    bpt = KV_TILE // SEL_BLOCK

    def branch(k_ref, vt_ref, j_lo, j_hi, allowed_fn):
        m_ref[...] = jnp.full(m_ref.shape, NEG, F32)
        l_ref[...] = jnp.zeros(l_ref.shape, F32)
        acc_ref[...] = jnp.zeros(acc_ref.shape, F32)

        def body(j, carry):
            k0 = pl.multiple_of(j * KV_TILE, KV_TILE)
            s_t = jnp.dot(k_ref[pl.ds(k0, KV_TILE), :], qt, preferred_element_type=F32)
            spos = k0 + _iota((KV_TILE, 1), 0)
            allowed = allowed_fn(j, spos)
            dist = (tq - spos).astype(F32)
            parts = []
            for g in range(NSA_GROUP):
                sl = slice(g * Q_TILE, (g + 1) * Q_TILE)
                parts.append(jnp.where(allowed, s_t[:, sl] - slope[:, sl] * dist, NEG))
            s = jnp.concatenate(parts, axis=1)
            alpha, p = _online_update(s, None, m_ref, l_ref, acc_ref, None)
            acc_ref[...] = alpha * acc_ref[...] + jnp.dot(
                vt_ref[:, pl.ds(k0, KV_TILE)], p.astype(BF16), preferred_element_type=F32)
            return carry

        lax.fori_loop(j_lo, j_hi, body, 0)
        return acc_ref[...] / l_ref[...]

    def slc_allowed(j, spos):
        rows = mask_ref[pl.ds(pl.multiple_of(j * bpt, bpt), bpt), :]
        return (_expand_rows(rows, SEL_BLOCK) > 0.5) & (spos <= tq)

    def win_allowed(j, spos):
        return (spos <= tq) & (spos > tq - WINDOW)

    n_slc = (t0 + Q_TILE + KV_TILE - 1) // KV_TILE
    o_s = branch(ks_ref, vst_ref, 0, n_slc, slc_allowed)
    j_hi = t0 // KV_TILE + 1
    j_lo = jnp.maximum(j_hi - 1 - WINDOW // KV_TILE, 0)
    o_w = branch(kw_ref, vwt_ref, j_lo, j_hi, win_allowed)
    gates = jax.nn.sigmoid(gl_ref[...])
    o_ref[...] = gates[0:1, :] * oc_ref[...] + gates[1:2, :] * o_s + gates[2:3, :] * o_w


def nsa_slc_prompt(qt, ks, vst, kw, vwt, mask_t, oc_t, gl_t, slopes):
    nkv, nq, hd, lanes = qt.shape
    t = ks.shape[1]
    nsp = mask_t.shape[1]
    tile = lambda: pl.BlockSpec((None, None, hd, lanes), lambda k, i: (k, i, 0, 0))
    return pl.pallas_call(
        _nsa_slc_p_kernel,
        grid=(nkv, nq),
        in_specs=[tile(),
                  pl.BlockSpec((None, t, hd), lambda k, i: (k, 0, 0)),
                  pl.BlockSpec((None, hd, t), lambda k, i: (k, 0, 0)),
                  pl.BlockSpec((None, t, hd), lambda k, i: (k, 0, 0)),
                  pl.BlockSpec((None, hd, t), lambda k, i: (k, 0, 0)),
                  pl.BlockSpec((None, nsp, Q_TILE), lambda k, i: (k, 0, i)),
                  tile(),
                  pl.BlockSpec((None, None, 3, lanes), lambda k, i: (k, i, 0, 0)),
                  pl.BlockSpec((None, 1, lanes), lambda k, i: (k, 0, 0))],
        out_specs=tile(),
        out_shape=jax.ShapeDtypeStruct((nkv, nq, hd, lanes), F32),
        scratch_shapes=[pltpu.VMEM((1, lanes), F32),
                        pltpu.VMEM((1, lanes), F32),
                        pltpu.VMEM((hd, lanes), F32)],
        compiler_params=_cparams(("parallel", "arbitrary")),
        name="nsa_slc_prompt",
    )(qt, ks, vst, kw, vwt, mask_t, oc_t, gl_t, slopes)


def _diag_blocks(o_full, rows_per_head):
    return jnp.concatenate(
        [o_full[k * rows_per_head:(k + 1) * rows_per_head, k * NSA_HD:(k + 1) * NSA_HD]
         for k in range(NSA_KV_HEADS)], axis=0)


def _nsa_cmp_s_kernel(n_top, qbd_ref, kc_ref, vc_ref, selmap_ref, gsum_ref, slope_ref, tpos_ref,
                      oc_ref, mask_ref):
    tpos = tpos_ref[...]
    p = _cmp_softmax(_mm(kc_ref[...], qbd_ref[...]), slope_ref[...], tpos)
    lanes = p.shape[1]
    oc_ref[...] = _diag_blocks(_mm_tn(p, vc_ref[...]), lanes // NSA_KV_HEADS)
    imp = _mm_split_l(_mm_split(selmap_ref[...], p), gsum_ref[...])
    mask_ref[...] = _select_blocks(imp, tpos, n_top)


def nsa_cmp_sample(qbd, cmp_kv, selmap_t, gsum, slope_row, tpos_row, ls):
    b, width, lanes = qbd.shape
    ncp = cmp_kv.shape[1]
    nsp = selmap_t.shape[0]
    n_top = min(SEL_TOP, -(-ls // SEL_BLOCK))
    return pl.pallas_call(
        functools.partial(_nsa_cmp_s_kernel, n_top),
        grid=(b,),
        in_specs=[pl.BlockSpec((None, width, lanes), lambda bi: (bi, 0, 0)),
                  pl.BlockSpec((None, ncp, width), lambda bi: (bi, 0, 0)),
                  pl.BlockSpec((None, ncp, width), lambda bi: (bi, 0, 1)),
                  pl.BlockSpec((nsp, ncp), lambda bi: (0, 0)),
                  pl.BlockSpec((lanes, lanes), lambda bi: (0, 0)),
                  pl.BlockSpec((1, lanes), lambda bi: (0, 0)),
                  pl.BlockSpec((1, lanes), lambda bi: (0, 0))],
        out_specs=[pl.BlockSpec((None, lanes, NSA_HD), lambda bi: (bi, 0, 0)),
                   pl.BlockSpec((None, nsp, lanes), lambda bi: (bi, 0, 0))],
        out_shape=[jax.ShapeDtypeStruct((b, lanes, NSA_HD), F32),
                   jax.ShapeDtypeStruct((b, nsp, lanes), F32)],
        compiler_params=_cparams(("parallel",)),
        name="nsa_cmp_sample",
    )(qbd, cmp_kv, cmp_kv, selmap_t, gsum, slope_row, tpos_row)


def _nsa_slc_s_kernel(npg, past, w_start, pt_ref, *refs):
    pages = refs[:npg]
    (qbd_ref, mask_ref, new_ref, win_ref, oc_ref, gl_ref, slope_ref, tpos_ref,
     o_ref, m_ref, l_ref, acc_ref) = refs[npg:]
    s = pl.program_id(1)
    ns = pl.num_programs(1)
    qbd = qbd_ref[...]
    width = qbd.shape[0]
    lanes = qbd.shape[1]
    tpos = tpos_ref[...]
    tf = tpos.astype(F32)
    slope = slope_ref[...]
    page = pages[0].shape[0]
    bpp = page // SEL_BLOCK

    @pl.when(s == 0)
    def _init():
        m_ref[...] = jnp.full(m_ref.shape, NEG, F32)
        l_ref[...] = jnp.zeros(l_ref.shape, F32)
        acc_ref[...] = jnp.zeros(acc_ref.shape, F32)

    def scores(k_rows, pos0):
        n = k_rows.shape[0]
        spos = pos0 + _iota((n, 1), 0)
        return _mm(k_rows, qbd) - slope * (tf - spos.astype(F32)), spos

    def update(sc, v_rows):
        alpha, p = _online_update(sc, None, m_ref, l_ref, acc_ref, None)
        acc_ref[...] = _row_to_col(alpha) * acc_ref[...] + _mm_tn(p, v_rows)

    k_rows = jnp.concatenate([pg[:, 0:width] for pg in pages], axis=0)
    v_rows = jnp.concatenate([pg[:, width:2 * width] for pg in pages], axis=0)
    sc, spos = scores(k_rows, s * (npg * page))
    nblk = npg * bpp
    rows = mask_ref[pl.ds(pl.multiple_of(s * nblk, nblk), nblk), :]
    allowed = (_expand_rows(rows, SEL_BLOCK) > 0.5) & (spos <= tpos)
    update(jnp.where(allowed, sc, NEG), v_rows)

    @pl.when(s == ns - 1)
    def _finish():
        new = new_ref[...]
        sc_n, spos_n = scores(new[:, 0:width], past)
        blk0 = past // SEL_BLOCK
        row = mask_ref[pl.ds(blk0, 8), :][0:1, :]
        ok_n = (row > 0.5) & (spos_n <= tpos)
        update(jnp.where(ok_n, sc_n, NEG), new[:, width:2 * width])
        o_s = acc_ref[...] / _row_to_col(l_ref[...])
        win = win_ref[...]
        sc_w, wpos = scores(win[:, 0:width], w_start)
        ok_w = (wpos >= 0) & (wpos <= tpos) & (wpos > tpos - WINDOW)
        sc_w = jnp.where(ok_w, sc_w, NEG)
        p_w = jnp.exp(sc_w - jnp.max(sc_w, 0, keepdims=True))
        o_w = _mm_tn(p_w, win[:, width:2 * width]) / _row_to_col(jnp.sum(p_w, 0, keepdims=True))
        rph = lanes // NSA_KV_HEADS
        gates = jax.nn.sigmoid(gl_ref[...])
        o_ref[...] = (gates[:, 0:1] * oc_ref[...] + gates[:, 1:2] * _diag_blocks(o_s, rph)
                      + gates[:, 2:3] * _diag_blocks(o_w, rph))


def nsa_slc_sample(pages, page_table, qbd, mask_t, new_rows, win_all, oc, gl, slope_row, tpos_row,
                   past, w_start):
    b, n_pages = page_table.shape
    npg = SLC_PAGES
    page, width2 = pages.shape[1:]
    width, lanes = qbd.shape[1:]
    nsp = mask_t.shape[1]
    n_new = new_rows.shape[1]
    n_win = win_all.shape[1]
    page_specs = [pl.BlockSpec((None, page, width2), functools.partial(lambda r, bi, s, pt: (pt[bi, s * npg + r], 0, 0), r))
                  for r in range(npg)]
    per_b = lambda shape: pl.BlockSpec((None,) + shape, lambda bi, s, pt: (bi, 0, 0))
    const = lambda shape: pl.BlockSpec(shape, lambda bi, s, pt: (0, 0))
    grid_spec = pltpu.PrefetchScalarGridSpec(
        num_scalar_prefetch=1,
        grid=(b, n_pages // npg),
        in_specs=page_specs + [per_b((width, lanes)), per_b((nsp, lanes)), per_b((n_new, width2)),
                               per_b((n_win, width2)), per_b((lanes, NSA_HD)), per_b((lanes, 8)),
                               const((1, lanes)), const((1, lanes))],
        out_specs=per_b((lanes, NSA_HD)),
        scratch_shapes=[pltpu.VMEM((1, lanes), F32),
                        pltpu.VMEM((1, lanes), F32),
                        pltpu.VMEM((lanes, width), F32)],
    )
    return pl.pallas_call(
        functools.partial(_nsa_slc_s_kernel, npg, past, w_start),
        grid_spec=grid_spec,
        out_shape=jax.ShapeDtypeStruct((b, lanes, NSA_HD), F32),
        compiler_params=_cparams(("parallel", "arbitrary")),
        name="nsa_slc_sample",
    )(page_table, *([pages] * npg), qbd, mask_t, new_rows, win_all, oc, gl, slope_row, tpos_row)


def _alibi_slopes():
    return 2.0 ** (-8.0 * (np.arange(NSA_HEADS) + 1) / NSA_HEADS)


def _selmap_t(nc, ns, ns_pad, ncp):
    r = CMP_BLOCK // CMP_STRIDE
    rs = SEL_BLOCK // CMP_STRIDE
    d = np.arange(nc)[:, None] - rs * np.arange(ns)[None, :]
    m = sum(((d + n >= 0) & (d + n < rs)).astype(np.float32) for n in range(r))
    out = np.zeros((ns_pad, ncp), np.float32)
    out[:ns, 1:nc + 1] = m.T
    return jnp.asarray(out, BF16)


def _mod_rows(chunk, n_sample_rep):
    top = jnp.broadcast_to(chunk[0:1], (ROW_TILE, chunk.shape[1]))
    bottom = jnp.repeat(chunk[1:], n_sample_rep, axis=0)
    return jnp.concatenate([top, bottom], 0)


def nsa_prompt(q, gl, kv, cmp_kv, t):
    nkv, grp, hd = NSA_KV_HEADS, NSA_GROUP, NSA_HD
    nq = t // Q_TILE
    lanes = grp * Q_TILE
    half = nkv * hd
    to_t = lambda x: x.reshape(nq, Q_TILE, nkv, grp, -1)
    qt = (jnp.transpose(to_t(q), (2, 0, 4, 3, 1)).reshape(nkv, nq, hd, lanes) * hd ** -0.5).astype(BF16)
    gl_t = jnp.transpose(gl.reshape(nq, Q_TILE, 3, nkv, grp), (3, 0, 2, 4, 1)).reshape(nkv, nq, 3, lanes)
    heads = lambda x: jnp.transpose(x.reshape(t, nkv, hd), (1, 0, 2)).astype(BF16)
    heads_t = lambda x: jnp.transpose(x.reshape(t, nkv, hd), (1, 2, 0)).astype(BF16)
    ks, vst = heads(kv[:, 2 * half:3 * half]), heads_t(kv[:, 3 * half:4 * half])
    kw, vwt = heads(kv[:, 4 * half:5 * half]), heads_t(kv[:, 5 * half:6 * half])
    ncp = cmp_kv.shape[1]
    kc = jnp.transpose(cmp_kv[0, :, 0:half].reshape(ncp, nkv, hd), (1, 0, 2)).astype(BF16)
    vct = jnp.transpose(cmp_kv[0, :, half:2 * half].reshape(ncp, nkv, hd), (1, 2, 0)).astype(BF16)
    ns = -(-t // SEL_BLOCK)
    nsp = -(-ns // SUBLANES) * SUBLANES
    selmap_t = _selmap_t(ncp - 1, ns, nsp, ncp)
    slopes = jnp.asarray(np.repeat(_alibi_slopes().reshape(nkv, 1, grp, 1), Q_TILE, axis=3).reshape(nkv, 1, lanes), F32)
    oc_t, mask_t = nsa_cmp_prompt(qt, kc, vct, selmap_t, slopes, t)
    o_t = nsa_slc_prompt(qt, ks, vst, kw, vwt, mask_t, oc_t, gl_t, slopes)
    return jnp.transpose(o_t.reshape(nkv, nq, hd, grp, Q_TILE), (1, 4, 0, 3, 2)).reshape(t, nkv * grp * hd)


def nsa_sample(q, gl, kv_new, cmp_kv, slc_pages, page_table, win_all, b, l, past, w_start):
    nkv, grp, hd = NSA_KV_HEADS, NSA_GROUP, NSA_HD
    lanes = nkv * grp * l
    half = nkv * hd
    q5 = q.reshape(b, l, nkv, grp, hd) * hd ** -0.5
    qbd = jnp.einsum('btkgd,kj->bjdkgt', q5, jnp.eye(nkv, dtype=F32)).reshape(b, half, lanes).astype(BF16)
    gl_r = jnp.transpose(gl.reshape(b, l, 3, nkv, grp), (0, 3, 4, 1, 2)).reshape(b, lanes, 3)
    gl_r = jnp.pad(gl_r, ((0, 0), (0, 0), (0, 5)))
    slope_row = jnp.asarray(np.repeat(_alibi_slopes(), l).reshape(1, lanes), F32)
    tpos_row = jnp.asarray(np.tile(past + np.arange(l), nkv * grp).reshape(1, lanes), I32)
    ls = past + l
    ncp = cmp_kv.shape[1]
    ns = -(-ls // SEL_BLOCK)
    nsp = -(-(ns + SUBLANES) // SUBLANES) * SUBLANES
    selmap_t = _selmap_t(ncp - 1, ns, nsp, ncp)
    lane = np.arange(lanes)
    same = (lane[:, None] // (grp * l) == lane[None, :] // (grp * l)) & (lane[:, None] % l == lane[None, :] % l)
    gsum = jnp.asarray(same.astype(np.float32), BF16)
    oc, mask_t = nsa_cmp_sample(qbd, cmp_kv, selmap_t, gsum, slope_row, tpos_row, ls)
    new_rows = kv_new[:, 2 * half:4 * half].reshape(b, l, 2 * half)
    o = nsa_slc_sample(slc_pages, page_table, qbd, mask_t, new_rows, win_all, oc, gl_r, slope_row, tpos_row,
                       past, w_start)
    return jnp.transpose(o.reshape(b, nkv, grp, l, hd), (0, 3, 1, 2, 4)).reshape(b * l, nkv * grp * hd)


def kernel(x_prompt, x_sample, state_gdn, state_conv, cache_cmp_kv, cache_slc_kv, cache_win_kv, page_table,
           c_prompt, c_sample, ada_w, ada_b, norm_mix, norm_ffn, gdn_w_in, gdn_conv_w, gdn_a_log, gdn_dt_bias,
           gdn_norm, gdn_w_out, kv_ada_w, kv_ada_b, kv_norm, kv_w, cmp_pe, cmp_w1, cmp_w2, nsa_w_in, nsa_w_out,
           router_w, router_b, moe_w_gu, moe_b_gu, moe_w_dn, moe_b_dn, norm_f):
    bp, seq, d = x_prompt.shape
    db, dl, _ = x_sample.shape
    assert bp == 1 and db * dl == ROW_TILE and seq % ROW_TILE == 0 and ada_w.shape[0] == 2
    n_pt = seq // ROW_TILE
    page = cache_cmp_kv.shape[1]
    past = page_table.shape[1] * page
    w_buf = cache_win_kv.shape[1]
    w_start = past - w_buf
    nkv, hd = NSA_KV_HEADS, NSA_HD
    half = nkv * hd
    kvw = 2 * half
    hw = GDN_HEADS * GDN_DK
    qkvw = 3 * hw

    c_all = jnp.concatenate([c_prompt, c_sample], 0)
    n_c = c_all.shape[0]
    c_pad = jnp.pad(c_all, ((0, -n_c % SUBLANES), (0, 0)))
    rows6 = lambda mod: [_mod_rows(ch, dl) for ch in jnp.split(mod[:n_c], mod.shape[1] // d, -1)]
    mod0 = rows6(cond_matmul(c_pad, ada_w[0], ada_b[0]))
    mod1 = rows6(cond_matmul(c_pad, ada_w[1], ada_b[1]))
    kv_sh, kv_sc = rows6(cond_matmul(c_pad, kv_ada_w, kv_ada_b))

    h = jnp.concatenate([x_prompt.reshape(seq, d), x_sample.reshape(db * dl, d)], 0)

    sh1, sc1, g1, sh2, sc2, g2 = mod0
    w_in = gdn_w_in[0]
    w_in = jnp.pad(w_in, ((0, 0), (0, -w_in.shape[1] % LANES))).astype(BF16)
    (proj,) = norm_proj(h, [(norm_mix[0], sh1, sc1, w_in)], n_pt)
    zeros_s = jnp.zeros((1,) + state_gdn.shape[2:], F32)
    zeros_c = jnp.zeros((1, 8, qkvw), F32)
    conv_s = jnp.pad(state_conv[0], ((0, 0), (8 - (GDN_CONV - 1), 0), (0, 0)))
    gdn_args = (gdn_conv_w[0], gdn_a_log[0], gdn_dt_bias[0], gdn_norm[0])
    o_p, p_gdn, p_conv = gdn(proj, 0, 1, seq, min(GDN_CHUNK, seq), zeros_s, zeros_c, *gdn_args)
    o_s, s_gdn, s_conv = gdn(proj, seq, db, dl, min(GDN_CHUNK, dl), state_gdn[0], conv_s, *gdn_args)
    h = proj_residual(jnp.concatenate([o_p, o_s], 0), gdn_w_out[0].astype(BF16), h, g1, n_pt)

    def expert_weights(layer):
        w_dn = moe_w_dn[layer]
        wd = jnp.stack([w_dn, jnp.zeros_like(w_dn)], 2).reshape(w_dn.shape[0], 2 * w_dn.shape[1], w_dn.shape[2])
        return (moe_w_gu[layer].astype(BF16), moe_b_gu[layer][:, None, :], wd.astype(BF16), moe_b_dn[layer][:, None, :])

    h = moe_layer(h, norm_ffn[0], sh2, sc2, g2, router_w[0], router_b[0], *expert_weights(0), n_pt)[0]

    sh1, sc1, g1, sh2, sc2, g2 = mod1
    w_q = nsa_w_in[0]
    w_q = jnp.pad(w_q, ((0, 0), (0, -w_q.shape[1] % LANES))).astype(BF16)
    kvp, qp = norm_proj(h, [(kv_norm, kv_sh, kv_sc, kv_w.astype(BF16)), (norm_mix[1], sh1, sc1, w_q)], n_pt)
    nq_cols = NSA_HEADS * hd
    q_all, gl_all = qp[:, :nq_cols], qp[:, nq_cols:nq_cols + 3 * NSA_HEADS]
    kv_p, kv_s = kvp[:seq], kvp[seq:]
    p_cmp, p_slc, p_win = kv_p[:, 0:kvw], kv_p[:, kvw:2 * kvw], kv_p[:, 2 * kvw:3 * kvw]
    s_cmp, s_slc, s_win_new = kv_s[:, 0:kvw], kv_s[:, kvw:2 * kvw], kv_s[:, 2 * kvw:3 * kvw]

    pe_t = jnp.tile(cmp_pe, (1, 1, nkv))
    w1_bd = _block_diag4(cmp_w1).astype(BF16)
    w2_bd = _block_diag4(cmp_w2).astype(BF16)
    cmp_p = compress(p_cmp.reshape(seq // page, page, kvw), jnp.arange(seq // page, dtype=I32).reshape(1, -1),
                     pe_t, w1_bd, w2_bd)
    cmp_s = compress(cache_cmp_kv.reshape(-1, page, kvw), page_table, pe_t, w1_bd, w2_bd)

    win_all = jnp.concatenate([cache_win_kv.reshape(db, w_buf, kvw), s_win_new.reshape(db, dl, kvw)], 1)
    o_p = nsa_prompt(q_all[:seq], gl_all[:seq], kv_p, cmp_p, seq)
    o_s = nsa_sample(q_all[seq:], gl_all[seq:], kv_s, cmp_s, cache_slc_kv.reshape(-1, page, kvw), page_table,
                     win_all, db, dl, past, w_start)
    h = proj_residual(jnp.concatenate([o_p, o_s], 0), nsa_w_out[0].astype(BF16), h, g1, n_pt)
    h, y = moe_layer(h, norm_ffn[1], sh2, sc2, g2, router_w[1], router_b[1], *expert_weights(1), n_pt, norm_f)

    kv5 = lambda x, b_: x.reshape(b_, -1, 2, nkv, hd)
    win_keep = lambda x: x[:, max(0, x.shape[1] - WINDOW):]
    return (y[:seq].reshape(bp, seq, d), y[seq:].reshape(db, dl, d),
            p_gdn[None], p_conv[None],
            kv5(p_cmp, bp), kv5(p_slc, bp), win_keep(kv5(p_win, bp)),
            s_gdn[None], s_conv[None],
            kv5(s_cmp, db), kv5(s_slc, db), win_keep(kv5(win_all, db)))
```

```python
import functools

import jax
import jax.numpy as jnp
import numpy as np
from jax import lax
from jax.experimental import pallas as pl
from jax.experimental.pallas import tpu as pltpu

F32 = jnp.float32
BF16 = jnp.bfloat16
I32 = jnp.int32

GDN_HEADS = 8
GDN_DK = 128
GDN_DV = 128
GDN_CONV = 4
GDN_CHUNK = 64
NSA_HEADS = 16
NSA_KV_HEADS = 4
NSA_GROUP = NSA_HEADS // NSA_KV_HEADS
NSA_HD = 64
CMP_BLOCK = 32
CMP_STRIDE = 16
SEL_BLOCK = 64
SEL_TOP = 16
SEL_FORCE = 1000.0
WINDOW = 512
TOP_K = 4
SWIGLU_LIMIT = 7.0
SWIGLU_ALPHA = 1.702
EPS = 1e-6
NEG = -1e30

LANES = 128
SUBLANES = 8
ROW_TILE = 256
MOE_TILE = 256
KV_TILE = 512
Q_TILE = 128
CMP_PAGES = 16
CMP_ROW_CHUNK = 256
SLC_PAGES = 8
VMEM_LIMIT = 56 * 1024 * 1024
HIGHEST = lax.Precision.HIGHEST


def _cparams(sem):
    return pltpu.CompilerParams(dimension_semantics=sem, vmem_limit_bytes=VMEM_LIMIT)


def _mm(a, b):
    return jnp.dot(a.astype(BF16), b.astype(BF16), preferred_element_type=F32)


def _mm_nt(a, b):
    return lax.dot_general(a.astype(BF16), b.astype(BF16), (((1,), (1,)), ((), ())),
                           preferred_element_type=F32)


def _mm_tn(a, b):
    return lax.dot_general(a.astype(BF16), b.astype(BF16), (((0,), (0,)), ((), ())),
                           preferred_element_type=F32)


def _mm32(a, b):
    return jnp.dot(a, b, precision=HIGHEST, preferred_element_type=F32)


def _mm_split(a_exact_bf16, p):
    p_hi = p.astype(BF16)
    p_lo = (p - p_hi.astype(F32)).astype(BF16)
    return (jnp.dot(a_exact_bf16, p_hi, preferred_element_type=F32)
            + jnp.dot(a_exact_bf16, p_lo, preferred_element_type=F32))


def _mm_split_l(p, b_exact_bf16):
    p_hi = p.astype(BF16)
    p_lo = (p - p_hi.astype(F32)).astype(BF16)
    return (jnp.dot(p_hi, b_exact_bf16, preferred_element_type=F32)
            + jnp.dot(p_lo, b_exact_bf16, preferred_element_type=F32))


def _row_to_col(row):
    n = row.shape[1]
    eye = _iota((n, n), 0) == _iota((n, n), 1)
    return jnp.sum(jnp.where(eye, jnp.broadcast_to(row, (n, n)), 0.0), axis=1, keepdims=True)


def _silu(x):
    return x * jax.nn.sigmoid(x)


def _iota(shape, dim):
    return lax.broadcasted_iota(I32, shape, dim)


def _cond_kernel(c_ref, w_ref, b_ref, o_ref):
    o_ref[...] = _mm(_silu(c_ref[...]), w_ref[...]) + b_ref[...]


def cond_matmul(c, w, b, tn=1024):
    m, d = c.shape
    n = w.shape[1]
    return pl.pallas_call(
        _cond_kernel,
        grid=(n // tn,),
        in_specs=[pl.BlockSpec((m, d), lambda j: (0, 0)),
                  pl.BlockSpec((d, tn), lambda j: (0, j)),
                  pl.BlockSpec((1, tn), lambda j: (0, j))],
        out_specs=pl.BlockSpec((m, tn), lambda j: (0, j)),
        out_shape=jax.ShapeDtypeStruct((m, n), F32),
        compiler_params=_cparams(("arbitrary",)),
        name="cond_matmul",
    )(c, w, b.reshape(1, n))


def _mod_map(n_prompt_tiles):
    return lambda i: (jnp.where(i < n_prompt_tiles, 0, 1), 0)


def _norm_proj_kernel(transposed, x_ref, *refs):
    n_heads = len(transposed)
    x = x_ref[...]
    xn = x * lax.rsqrt(jnp.mean(x * x, -1, keepdims=True) + EPS)
    for i in range(n_heads):
        g_ref, sh_ref, sc_ref, w_ref = refs[4 * i:4 * i + 4]
        o_ref = refs[4 * n_heads + i]
        u = (xn * g_ref[...]) * (1.0 + sc_ref[...]) + sh_ref[...]
        o_ref[...] = _mm_nt(w_ref[...], u) if transposed[i] else _mm(u, w_ref[...])


def norm_proj(h, heads, n_prompt_tiles):
    t, d = h.shape
    tm = ROW_TILE
    in_specs = [pl.BlockSpec((tm, d), lambda i: (i, 0))]
    args = [h]
    out_specs, out_shapes = [], []
    for gamma, sh, sc, w, transposed in heads:
        in_specs += [pl.BlockSpec((1, d), lambda i: (0, 0)),
                     pl.BlockSpec((tm, d), _mod_map(n_prompt_tiles)),
                     pl.BlockSpec((tm, d), _mod_map(n_prompt_tiles)),
                     pl.BlockSpec(w.shape, lambda i: (0, 0))]
        args += [gamma.reshape(1, d), sh, sc, w]
        if transposed:
            out_specs.append(pl.BlockSpec((w.shape[0], tm), lambda i: (0, i)))
            out_shapes.append(jax.ShapeDtypeStruct((w.shape[0], t), F32))
        else:
            out_specs.append(pl.BlockSpec((tm, w.shape[1]), lambda i: (i, 0)))
            out_shapes.append(jax.ShapeDtypeStruct((t, w.shape[1]), F32))
    return pl.pallas_call(
        functools.partial(_norm_proj_kernel, tuple(hd[4] for hd in heads)),
        grid=(t // tm,),
        in_specs=in_specs,
        out_specs=out_specs,
        out_shape=out_shapes,
        compiler_params=_cparams(("parallel",)),
        name="norm_proj",
    )(*args)


def _proj_res_t_kernel(n_prompt_tiles, ap_ref, as_ref, w_ref, h_ref, g_ref, o_ref):
    i = pl.program_id(0)

    @pl.when(i < n_prompt_tiles)
    def _prompt():
        o_ref[...] = h_ref[...] + g_ref[...] * _mm_tn(ap_ref[...], w_ref[...])

    @pl.when(i >= n_prompt_tiles)
    def _sample():
        o_ref[...] = h_ref[...] + g_ref[...] * _mm_tn(as_ref[...], w_ref[...])


def proj_residual_t(a_prompt_t, a_sample_t, w, h, gate_rows, n_prompt_tiles):
    t, d = h.shape
    k = w.shape[0]
    tm = ROW_TILE
    return pl.pallas_call(
        functools.partial(_proj_res_t_kernel, n_prompt_tiles),
        grid=(t // tm,),
        in_specs=[pl.BlockSpec((k, tm), lambda i: (0, jnp.minimum(i, n_prompt_tiles - 1))),
                  pl.BlockSpec((k, tm), lambda i: (0, 0)),
                  pl.BlockSpec((k, d), lambda i: (0, 0)),
                  pl.BlockSpec((tm, d), lambda i: (i, 0)),
                  pl.BlockSpec((tm, d), _mod_map(n_prompt_tiles))],
        out_specs=pl.BlockSpec((tm, d), lambda i: (i, 0)),
        out_shape=jax.ShapeDtypeStruct((t, d), F32),
        compiler_params=_cparams(("parallel",)),
        name="proj_residual_t",
    )(a_prompt_t, a_sample_t, w, h, gate_rows)


def _proj_res_kernel(a_ref, w_ref, h_ref, g_ref, o_ref):
    o_ref[...] = h_ref[...] + g_ref[...] * _mm(a_ref[...], w_ref[...])


def proj_residual(a, w, h, gate_rows, n_prompt_tiles):
    t, k = a.shape
    d = w.shape[1]
    tm = ROW_TILE
    return pl.pallas_call(
        _proj_res_kernel,
        grid=(t // tm,),
        in_specs=[pl.BlockSpec((tm, k), lambda i: (i, 0)),
                  pl.BlockSpec((k, d), lambda i: (0, 0)),
                  pl.BlockSpec((tm, d), lambda i: (i, 0)),
                  pl.BlockSpec((tm, d), _mod_map(n_prompt_tiles))],
        out_specs=pl.BlockSpec((tm, d), lambda i: (i, 0)),
        out_shape=jax.ShapeDtypeStruct((t, d), F32),
        compiler_params=_cparams(("parallel",)),
        name="proj_residual",
    )(a, w, h, gate_rows)


def _mm3(a, b):
    a_hi = a.astype(BF16)
    b_hi = b.astype(BF16)
    a_lo = (a - a_hi.astype(F32)).astype(BF16)
    b_lo = (b - b_hi.astype(F32)).astype(BF16)
    dot = lambda x, y: jnp.dot(x, y, preferred_element_type=F32)
    return dot(a_hi, b_hi) + (dot(a_hi, b_lo) + dot(a_lo, b_hi))


def _tri_inv(ms, c):
    r = _iota((c, c), 0)
    col = _iota((c, c), 1)
    eye = (r == col).astype(F32)
    same8 = (r // 8) == (col // 8)
    ds = [jnp.where(same8, m, 0.0) for m in ms]
    d2s = [_mm3(d, d) for d in ds]
    xs = [eye - d for d in ds]
    d4s = [_mm3(d2, d2) for d2 in d2s]
    xs = [x + _mm3(x, d2) for x, d2 in zip(xs, d2s)]
    xs = [x + _mm3(x, d4) for x, d4 in zip(xs, d4s)]
    size = 8
    while size < c:
        off = ((r // (2 * size)) == (col // (2 * size))) & ((r // size) != (col // size))
        ys = [_mm3(x, jnp.where(off, m, 0.0)) for x, m in zip(xs, ms)]
        xs = [x - _mm3(y, x) for x, y in zip(xs, ys)]
        size *= 2
    return xs


def _cumsum_rows(x):
    r = _iota(x.shape, 0)
    sh = 1
    while sh < x.shape[0]:
        x = x + jnp.where(r >= sh, pltpu.roll(x, sh, 0), 0.0)
        sh *= 2
    return x


def _gdn_kernel(c, q_ref, k_ref, v_ref, z_ref, ba_ref, conv0_ref, s0_ref, cw_ref, alog_ref, dtb_ref, ng_ref,
                o_ref, sfin_ref, convn_ref, xs_ref, xc_ref, st_ref):
    n = pl.program_id(1)
    nh, dk = GDN_HEADS, GDN_DK
    hw = nh * dk

    @pl.when(n == 0)
    def _init():
        xs_ref[0:8, :] = conv0_ref[...]
        st_ref[...] = s0_ref[...]

    xs_ref[8:8 + c, 0:hw] = q_ref[...]
    xs_ref[8:8 + c, hw:2 * hw] = k_ref[...]
    xs_ref[8:8 + c, 2 * hw:3 * hw] = v_ref[...]
    acc = cw_ref[3:4, :] * xs_ref[8:8 + c, :]
    for j in range(GDN_CONV - 1):
        acc = acc + cw_ref[j:j + 1, :] * xs_ref[5 + j:5 + j + c, :]
    xc_ref[...] = _silu(acc)
    convn_ref[...] = xs_ref[5 + c:8 + c, :]
    xs_ref[0:8, :] = xs_ref[c:c + 8, :]

    ba = ba_ref[...]
    beta_all = jax.nn.sigmoid(ba)
    xa = ba + dtb_ref[...]
    softplus = jnp.maximum(xa, 0.0) + jnp.log(1.0 + jnp.exp(-jnp.abs(xa)))
    g_all = -jnp.exp(alog_ref[...]) * softplus

    r = _iota((c, c), 0)
    col = _iota((c, c), 1)
    incl = r >= col
    strict = r > col
    eye = r == col
    heads = range(nh)
    qh = [xc_ref[:, h * dk:(h + 1) * dk] for h in heads]
    kh = [xc_ref[:, hw + h * dk:hw + (h + 1) * dk] for h in heads]
    vh = [xc_ref[:, 2 * hw + h * dk:2 * hw + (h + 1) * dk] for h in heads]
    qn = [x * lax.rsqrt(jnp.sum(x * x, -1, keepdims=True) + EPS) * (dk ** -0.5) for x in qh]
    kn = [x * lax.rsqrt(jnp.sum(x * x, -1, keepdims=True) + EPS) for x in kh]
    beta = [beta_all[:, h:h + 1] for h in heads]
    gc_b = [_cumsum_rows(jnp.broadcast_to(g_all[:, nh + h:nh + h + 1], (c, dk))) for h in heads]
    gc_row = [jnp.sum(jnp.where(eye, g[:, 0:c], 0.0), axis=0, keepdims=True) for g in gc_b]
    decay = [jnp.where(incl, jnp.exp(jnp.where(incl, g[:, 0:c] - gr, 0.0)), 0.0) for g, gr in zip(gc_b, gc_row)]
    kb = [k * b for k, b in zip(kn, beta)]
    vb = [v * b for v, b in zip(vh, beta)]
    kk = [_mm_nt(a, k) for a, k in zip(kb, kn)]
    qk = [_mm_nt(q, k) for q, k in zip(qn, kn)]
    t_inv = _tri_inv([jnp.where(strict, x * dcy, 0.0) for x, dcy in zip(kk, decay)], c)
    attn = [jnp.where(incl, x * dcy, 0.0) for x, dcy in zip(qk, decay)]
    egc = [jnp.exp(g) for g in gc_b]
    u = [_mm(t, v) for t, v in zip(t_inv, vb)]
    w = [_mm(t, a * e) for t, a, e in zip(t_inv, kb, egc)]
    gl_b = [g[c - 1:c, :] for g in gc_b]
    kg = [k * jnp.exp(gl - g) for k, gl, g in zip(kn, gl_b, gc_b)]
    s = [st_ref[h] for h in heads]
    ws = [_mm(a, b) for a, b in zip(w, s)]
    o1 = [_mm(q * e, b) for q, e, b in zip(qn, egc, s)]
    v_new = [a - b for a, b in zip(u, ws)]
    o2 = [_mm(a, v) for a, v in zip(attn, v_new)]
    kv = [_mm_tn(k, v) for k, v in zip(kg, v_new)]
    for h in heads:
        s_new = s[h] * jnp.exp(gl_b[h]) + kv[h]
        st_ref[h] = s_new
        sfin_ref[h] = s_new
        o = o1[h] + o2[h]
        on = o * lax.rsqrt(jnp.mean(o * o, -1, keepdims=True) + EPS) * ng_ref[...]
        o_ref[:, h * dk:(h + 1) * dk] = on * _silu(z_ref[:, h * dk:(h + 1) * dk])


def gdn(proj, row_off, b, l, c, s0, conv0, conv_w, a_log, dt_bias, norm_g):
    nh, dk = GDN_HEADS, GDN_DK
    hw = nh * dk
    nchunks = l // c
    rb0 = row_off // c
    row = lambda bi, n: rb0 + bi * nchunks + n
    gate_blk = 4 * hw // LANES
    lane_pad = jnp.zeros((LANES - 2 * nh,), F32)
    alog_row = jnp.concatenate([jnp.zeros((nh,), F32), a_log, lane_pad]).reshape(1, LANES)
    dtb_row = jnp.concatenate([jnp.zeros((nh,), F32), dt_bias, lane_pad]).reshape(1, LANES)
    return pl.pallas_call(
        functools.partial(_gdn_kernel, c),
        grid=(b, nchunks),
        in_specs=[pl.BlockSpec((c, hw), lambda bi, n: (row(bi, n), 0)),
                  pl.BlockSpec((c, hw), lambda bi, n: (row(bi, n), 1)),
                  pl.BlockSpec((c, hw), lambda bi, n: (row(bi, n), 2)),
                  pl.BlockSpec((c, hw), lambda bi, n: (row(bi, n), 3)),
                  pl.BlockSpec((c, LANES), lambda bi, n: (row(bi, n), gate_blk)),
                  pl.BlockSpec((None, 8, 3 * hw), lambda bi, n: (bi, 0, 0)),
                  pl.BlockSpec((None, nh, dk, GDN_DV), lambda bi, n: (bi, 0, 0, 0)),
                  pl.BlockSpec((GDN_CONV, 3 * hw), lambda bi, n: (0, 0)),
                  pl.BlockSpec((1, LANES), lambda bi, n: (0, 0)),
                  pl.BlockSpec((1, LANES), lambda bi, n: (0, 0)),
                  pl.BlockSpec((1, GDN_DV), lambda bi, n: (0, 0))],
        out_specs=[pl.BlockSpec((c, hw), lambda bi, n: (bi * nchunks + n, 0)),
                   pl.BlockSpec((None, nh, dk, GDN_DV), lambda bi, n: (bi, 0, 0, 0)),
                   pl.BlockSpec((None, GDN_CONV - 1, 3 * hw), lambda bi, n: (bi, 0, 0))],
        out_shape=[jax.ShapeDtypeStruct((b * l, hw), F32),
                   jax.ShapeDtypeStruct((b, nh, dk, GDN_DV), F32),
                   jax.ShapeDtypeStruct((b, GDN_CONV - 1, 3 * hw), F32)],
        scratch_shapes=[pltpu.VMEM((c + 8, 3 * hw), F32),
                        pltpu.VMEM((c, 3 * hw), F32),
                        pltpu.VMEM((nh, dk, GDN_DV), F32)],
        compiler_params=_cparams(("parallel", "arbitrary")),
        name="gdn",
    )(proj, proj, proj, proj, proj, conv0, s0, conv_w, alog_row, dtb_row, norm_g.reshape(1, GDN_DV))


def _route_kernel(x_ref, g_ref, sh_ref, sc_ref, rw_ref, rb_ref, u_ref, e_ref, w_ref):
    x = x_ref[...]
    xn = x * lax.rsqrt(jnp.mean(x * x, -1, keepdims=True) + EPS)
    u = (xn * g_ref[...]) * (1.0 + sc_ref[...]) + sh_ref[...]
    u_ref[...] = u
    logits = _mm32(u, rw_ref[...]) + rb_ref[...]
    tm, ne = logits.shape
    eidx = _iota((tm, ne), 1)
    lane = _iota((tm, LANES), 1)
    e_out = jnp.zeros((tm, LANES), I32)
    vals = []
    for k in range(TOP_K):
        m = jnp.max(logits, -1, keepdims=True)
        idx = jnp.min(jnp.where(logits == m, eidx, ne), -1, keepdims=True)
        logits = jnp.where(eidx == idx, -3e38, logits)
        vals.append(m)
        e_out = jnp.where(lane == k, idx, e_out)
    ex = [jnp.exp(v - vals[0]) for v in vals]
    den = ex[0] + ex[1] + ex[2] + ex[3]
    w_out = jnp.zeros((tm, LANES), F32)
    for k in range(TOP_K):
        w_out = jnp.where(lane == k, ex[k] / den, w_out)
    e_ref[...] = e_out
    w_ref[...] = w_out


def moe_route(h, gamma, sh, sc, router_w, router_b, n_prompt_tiles):
    t, d = h.shape
    ne = router_w.shape[1]
    tm = ROW_TILE
    return pl.pallas_call(
        _route_kernel,
        grid=(t // tm,),
        in_specs=[pl.BlockSpec((tm, d), lambda i: (i, 0)),
                  pl.BlockSpec((1, d), lambda i: (0, 0)),
                  pl.BlockSpec((tm, d), _mod_map(n_prompt_tiles)),
                  pl.BlockSpec((tm, d), _mod_map(n_prompt_tiles)),
                  pl.BlockSpec((d, ne), lambda i: (0, 0)),
                  pl.BlockSpec((1, ne), lambda i: (0, 0))],
        out_specs=[pl.BlockSpec((tm, d), lambda i: (i, 0)),
                   pl.BlockSpec((tm, LANES), lambda i: (i, 0)),
                   pl.BlockSpec((tm, LANES), lambda i: (i, 0))],
        out_shape=[jax.ShapeDtypeStruct((t, d), F32),
                   jax.ShapeDtypeStruct((t, LANES), I32),
                   jax.ShapeDtypeStruct((t, LANES), F32)],
        compiler_params=_cparams(("parallel",)),
        name="moe_route",
    )(h, gamma.reshape(1, d), sh, sc, router_w, router_b.reshape(1, ne))


def _expert_kernel(be_ref, nu_ref, tok0_ref, tokn_ref, dst_ref, x_hbm, wgu_ref, bgu_ref, wd_ref, bd_ref,
                   out_hbm, xbuf, ybuf, gsem, ssem):
    i = pl.program_id(0)
    n_used = nu_ref[0]
    slot = lax.rem(i, 2)
    tm = xbuf.shape[1]
    dump0 = out_hbm.shape[0] - 2 * tm

    def row_in(tok, s, j):
        return pltpu.make_async_copy(x_hbm.at[pl.ds(tok, 1)], xbuf.at[s, pl.ds(j, 1)], gsem.at[s])

    def row_out(dst, s, j):
        return pltpu.make_async_copy(ybuf.at[s, pl.ds(j, 1)], out_hbm.at[pl.ds(dst, 1)], ssem.at[s])

    def start_gather(tok_ref, s):
        for j in range(tm):
            row_in(tok_ref[0, j], s, j).start(priority=j % 2)

    def wait_gather(s):
        for j in range(tm):
            row_in(0, s, j).wait()

    def wait_scatter(s):
        for j in range(tm):
            row_out(0, s, j).wait()

    @pl.when(i == 0)
    def _first():
        start_gather(tok0_ref, 0)
        ybuf[...] = jnp.zeros(ybuf.shape, F32)
        for s in range(2):
            pltpu.make_async_copy(ybuf.at[s], out_hbm.at[pl.ds(dump0 + s * tm, tm)], ssem.at[s]).start()

    @pl.when(i < n_used)
    def _block():
        wait_gather(slot)
        start_gather(tokn_ref, 1 - slot)
        wait_scatter(slot)
        gu = _mm(xbuf[slot], wgu_ref[...]) + bgu_ref[...]
        parts = []
        for c in range(gu.shape[1] // LANES):
            pair = gu[:, c * LANES:(c + 1) * LANES]
            g = jnp.minimum(pair, SWIGLU_LIMIT)
            u = pltpu.roll(jnp.clip(pair, -SWIGLU_LIMIT, SWIGLU_LIMIT), LANES - 1, 1)
            parts.append((u + 1.0) * (g * jax.nn.sigmoid(SWIGLU_ALPHA * g)))
        ybuf[slot] = _mm(jnp.concatenate(parts, axis=1), wd_ref[...]) + bd_ref[...]
        for j in range(tm):
            row_out(dst_ref[0, j], slot, j).start(priority=j % 2)

        @pl.when(i == n_used - 1)
        def _drain():
            wait_gather(1 - slot)
            wait_scatter(slot)
            wait_scatter(1 - slot)


def moe_experts(u, row_tok, row_dst, blk_e, n_used, wgu, bgu, wd, bd, n_out_rows):
    t, d = u.shape
    ne, _, f2 = wgu.shape
    tm = MOE_TILE
    nb = row_tok.shape[0] // tm
    tok3 = row_tok.reshape(nb, 1, tm)
    dst3 = row_dst.reshape(nb, 1, tm)
    smem_blk = lambda imap: pl.BlockSpec((None, 1, tm), imap, memory_space=pltpu.SMEM)
    grid_spec = pltpu.PrefetchScalarGridSpec(
        num_scalar_prefetch=2,
        grid=(nb,),
        in_specs=[smem_blk(lambda i, be, nu: (i, 0, 0)),
                  smem_blk(lambda i, be, nu: (jnp.minimum(i + 1, nb - 1), 0, 0)),
                  smem_blk(lambda i, be, nu: (i, 0, 0)),
                  pl.BlockSpec(memory_space=pl.ANY),
                  pl.BlockSpec((None, d, f2), lambda i, be, nu: (be[i], 0, 0)),
                  pl.BlockSpec((None, 1, f2), lambda i, be, nu: (be[i], 0, 0)),
                  pl.BlockSpec((None, f2, d), lambda i, be, nu: (be[i], 0, 0)),
                  pl.BlockSpec((None, 1, d), lambda i, be, nu: (be[i], 0, 0))],
        out_specs=pl.BlockSpec(memory_space=pl.ANY),
        scratch_shapes=[pltpu.VMEM((2, tm, d), F32),
                        pltpu.VMEM((2, tm, d), F32),
                        pltpu.SemaphoreType.DMA((2,)),
                        pltpu.SemaphoreType.DMA((2,))],
    )
    return pl.pallas_call(
        _expert_kernel,
        grid_spec=grid_spec,
        out_shape=jax.ShapeDtypeStruct((n_out_rows, d), F32),
        compiler_params=_cparams(("arbitrary",)),
        name="moe_experts",
    )(blk_e, n_used, tok3, tok3, dst3, u, wgu, bgu, wd, bd)


def route_tables(top_e, n_exp):
    t = top_e.shape[0]
    tm = MOE_TILE
    m = t * TOP_K
    flat_e = top_e.reshape(-1)
    order = jnp.argsort(flat_e, stable=True).astype(I32)
    counts = jnp.sum((flat_e[:, None] == jnp.arange(n_exp, dtype=I32)[None, :]).astype(I32), 0)
    padded = (counts + tm - 1) // tm * tm
    start = jnp.cumsum(counts) - counts
    pend = jnp.cumsum(padded)
    pstart = pend - padded
    nb = -(-(m + n_exp * (tm - 1)) // tm)
    blk_e = jnp.minimum(jnp.searchsorted(pend, jnp.arange(nb, dtype=I32) * tm, side='right'), n_exp - 1).astype(I32)
    n_used = (pend[-1] // tm).astype(I32).reshape(1)
    per_row = lambda x: jnp.repeat(x[blk_e], tm)
    ridx = jnp.arange(nb * tm, dtype=I32)
    rank = ridx - per_row(pstart)
    valid = rank < per_row(counts)
    flat_src = order[jnp.where(valid, per_row(start) + rank, 0)]
    tok = flat_src // TOP_K
    slot = flat_src % TOP_K
    pad_dst = TOP_K * t + ((ridx // tm) % 2) * tm + ridx % tm
    row_tok = jnp.where(valid, tok, 0)
    row_dst = jnp.where(valid, slot * t + tok, pad_dst)
    return row_tok, row_dst, blk_e, n_used


def _combine_kernel(final, h_ref, g_ref, w_ref, s0_ref, s1_ref, s2_ref, s3_ref, *rest):
    w = w_ref[...]
    acc = (w[:, 0:1] * s0_ref[...] + w[:, 1:2] * s1_ref[...]
           + w[:, 2:3] * s2_ref[...] + w[:, 3:4] * s3_ref[...])
    hn = h_ref[...] + g_ref[...] * acc
    if final:
        nf_ref, o_ref, y_ref = rest
        y_ref[...] = hn * lax.rsqrt(jnp.mean(hn * hn, -1, keepdims=True) + EPS) * nf_ref[...]
    else:
        (o_ref,) = rest
    o_ref[...] = hn


def moe_combine(h, gate_rows, gate_w, slots, n_prompt_tiles, norm_f=None):
    t, d = h.shape
    tm = ROW_TILE
    nt = t // tm
    final = norm_f is not None
    in_specs = [pl.BlockSpec((tm, d), lambda i: (i, 0)),
                pl.BlockSpec((tm, d), _mod_map(n_prompt_tiles)),
                pl.BlockSpec((tm, LANES), lambda i: (i, 0))]
    in_specs += [pl.BlockSpec((tm, d), functools.partial(lambda k, i: (k * nt + i, 0), k)) for k in range(TOP_K)]
    args = [h, gate_rows, gate_w, slots, slots, slots, slots]
    out_specs = [pl.BlockSpec((tm, d), lambda i: (i, 0))]
    out_shape = [jax.ShapeDtypeStruct((t, d), F32)]
    if final:
        in_specs.append(pl.BlockSpec((1, d), lambda i: (0, 0)))
        args.append(norm_f.reshape(1, d))
        out_specs.append(pl.BlockSpec((tm, d), lambda i: (i, 0)))
        out_shape.append(jax.ShapeDtypeStruct((t, d), F32))
    return pl.pallas_call(
        functools.partial(_combine_kernel, final),
        grid=(nt,),
        in_specs=in_specs,
        out_specs=out_specs,
        out_shape=out_shape,
        compiler_params=_cparams(("parallel",)),
        name="moe_combine",
    )(*args)


def moe_layer(h, gamma, sh, sc, g2, router_w, router_b, wgu, bgu, wd, bd, n_prompt_tiles, norm_f=None):
    t = h.shape[0]
    u, top_e, gate_w = moe_route(h, gamma, sh, sc, router_w, router_b, n_prompt_tiles)
    row_tok, row_dst, blk_e, n_used = route_tables(top_e[:, :TOP_K], router_w.shape[1])
    slots = moe_experts(u, row_tok, row_dst, blk_e, n_used, wgu, bgu, wd, bd, TOP_K * t + 2 * MOE_TILE)
    return moe_combine(h, g2, gate_w, slots, n_prompt_tiles, norm_f)


def _compress_kernel(npg, pt_ref, *refs):
    pages = refs[:npg]
    pe_ref, w1_ref, w2_ref, out_ref, a_ref, xs_ref = refs[npg:]
    s = pl.program_id(1)
    page = pages[0].shape[0]
    m = npg * page // CMP_STRIDE
    half = NSA_KV_HEADS * NSA_HD
    nlb = half // LANES

    @pl.when(s == 0)
    def _init():
        a_ref[0:8, :] = jnp.zeros((8, 2 * half), F32)

    for r, pg in enumerate(pages):
        for cb in range(2 * nlb):
            xs_ref[cb, r * page:(r + 1) * page, :] = pg[:, cb * LANES:(cb + 1) * LANES]

    bms = []
    for c in range(2):
        acc_a = jnp.zeros((m, half), F32)
        acc_b = jnp.zeros((m, half), F32)
        for j in range(CMP_STRIDE):
            rows = jnp.concatenate(
                [xs_ref[c * nlb + cb, pl.ds(j, m, stride=CMP_STRIDE), :] for cb in range(nlb)], axis=1)
            acc_a = acc_a + _mm(rows + pe_ref[c, j:j + 1, :], w1_ref[c, j])
            acc_b = acc_b + _mm(rows + pe_ref[c, CMP_STRIDE + j:CMP_STRIDE + j + 1, :], w1_ref[c, CMP_STRIDE + j])
        a_ref[8:8 + m, c * half:(c + 1) * half] = acc_a
        bms.append(acc_b)
    for c in range(2):
        pre = a_ref[7:7 + m, c * half:(c + 1) * half] + bms[c]
        out_ref[:, c * half:(c + 1) * half] = _mm(_silu(pre), w2_ref[c])
    a_ref[0:8, :] = a_ref[m:m + 8, :]


def compress(pages, page_table, pe_t, w1_bd, w2_bd):
    b, n_pages = page_table.shape
    npg = CMP_PAGES
    page, width = pages.shape[1:]
    half = width // 2
    m = npg * page // CMP_STRIDE
    page_specs = [pl.BlockSpec((None, page, width), functools.partial(lambda r, bi, s, pt: (pt[bi, s * npg + r], 0, 0), r))
                  for r in range(npg)]
    grid_spec = pltpu.PrefetchScalarGridSpec(
        num_scalar_prefetch=1,
        grid=(b, n_pages // npg),
        in_specs=page_specs + [pl.BlockSpec((2, CMP_BLOCK, half), lambda bi, s, pt: (0, 0, 0)),
                               pl.BlockSpec((2, CMP_BLOCK, half, half), lambda bi, s, pt: (0, 0, 0, 0)),
                               pl.BlockSpec((2, half, half), lambda bi, s, pt: (0, 0, 0))],
        out_specs=pl.BlockSpec((None, m, width), lambda bi, s, pt: (bi, s, 0)),
        scratch_shapes=[pltpu.VMEM((m + 8, width), F32),
                        pltpu.VMEM((width // LANES, npg * page, LANES), F32)],
    )
    return pl.pallas_call(
        functools.partial(_compress_kernel, npg),
        grid_spec=grid_spec,
        out_shape=jax.ShapeDtypeStruct((b, n_pages * page // CMP_STRIDE, width), F32),
        compiler_params=_cparams(("parallel", "arbitrary")),
        name="compress",
    )(page_table, *([pages] * npg), pe_t, w1_bd, w2_bd)


def _block_diag4(w):
    eye = jnp.eye(NSA_KV_HEADS, dtype=w.dtype)
    out = jnp.einsum('ij,...ab->...iajb', eye, w)
    return out.reshape(w.shape[:-2] + (NSA_KV_HEADS * w.shape[-2], NSA_KV_HEADS * w.shape[-1]))


def _cmp_softmax(s_t, slope_row, tpos_row, exp=jnp.exp):
    ncp = s_t.shape[0]
    blk = _iota((ncp, 1), 0) - 1
    c_end = blk * CMP_STRIDE + (CMP_BLOCK - 1)
    c_ctr = (blk * CMP_STRIDE).astype(F32) + 0.5 * (CMP_BLOCK - 1)
    ok = (c_end <= tpos_row) & (blk >= 0)
    s = s_t - slope_row * (tpos_row.astype(F32) - c_ctr)
    s = jnp.where(ok, s, NEG)
    mx = jnp.max(s, 0, keepdims=True)
    p = jnp.where(ok, exp(s - mx), 0.0)
    den = jnp.sum(p, 0, keepdims=True)
    return p / jnp.where(den > 0.0, den, 1.0)


def _select_blocks(imp, tpos_row, n_top):
    sj = _iota(imp.shape, 0)
    cur = tpos_row // SEL_BLOCK
    forced = (sj == 0) | (sj == cur) | (sj == cur - 1)
    score = jnp.where(sj <= cur, jnp.where(forced, SEL_FORCE, imp), -1.0)
    ns = imp.shape[0]
    sel = jnp.zeros(imp.shape, F32)
    for _ in range(n_top):
        mx = jnp.max(score, 0, keepdims=True)
        idx = jnp.min(jnp.where(score == mx, sj, ns), 0, keepdims=True)
        pick = sj == idx
        sel = jnp.where(pick & (mx >= 0.0), 1.0, sel)
        score = jnp.where(pick, -2.0, score)
    return sel


def _expand_rows(mask_rows, rep):
    r = mask_rows.shape[0]
    e = (_iota((r * rep, r), 0) // rep == _iota((r * rep, r), 1)).astype(BF16)
    return jnp.dot(e, mask_rows.astype(BF16), preferred_element_type=F32)


def _online_update(s, m_ref, l_ref):
    m_old = m_ref[...]
    m_new = jnp.maximum(m_old, jnp.max(s, 0, keepdims=True))
    alpha = jnp.exp(m_old - m_new)
    p = jnp.exp(s - m_new)
    l_ref[...] = alpha * l_ref[...] + jnp.sum(p, 0, keepdims=True)
    m_ref[...] = m_new
    return alpha, p


def _group_queries(q_ref):
    blk = q_ref[...]
    return jnp.concatenate([blk[g * NSA_HD:(g + 1) * NSA_HD, :] for g in range(NSA_GROUP)], axis=1).astype(BF16)


def _nsa_cmp_p_kernel(n_top, q_ref, kc_ref, vct_ref, selmap_ref, slope_ref, oc_ref, mask_ref):
    i = pl.program_id(1)
    qt = _group_queries(q_ref)
    lanes = qt.shape[1]
    tpos = i * Q_TILE + _iota((1, lanes), 1) % Q_TILE
    ncp = kc_ref.shape[0]

    def attend(rows):
        p = _cmp_softmax(jnp.dot(kc_ref[0:rows, :], qt, preferred_element_type=F32), slope_ref[...], tpos,
                         jnp.exp2)
        oc_ref[...] = _mm(vct_ref[:, 0:rows], p)
        imp_g = _mm_split(selmap_ref[:, 0:rows], p)
        imp = imp_g[:, 0:Q_TILE]
        for g in range(1, NSA_GROUP):
            imp = imp + imp_g[:, g * Q_TILE:(g + 1) * Q_TILE]
        mask_ref[...] = _select_blocks(imp, tpos[:, 0:Q_TILE], n_top)

    need = (i * Q_TILE + Q_TILE - CMP_BLOCK) // CMP_STRIDE + 2
    chunk = CMP_ROW_CHUNK if ncp % CMP_ROW_CHUNK == 0 else ncp
    for c in range(ncp // chunk):
        last = c == ncp // chunk - 1
        cond = (need > c * chunk) if last else ((need > c * chunk) & (need <= (c + 1) * chunk))
        pl.when(cond)(functools.partial(attend, (c + 1) * chunk))


def nsa_cmp_prompt(q_t, kc, vct, selmap_t, slopes, t):
    nkv, ncp, hd = kc.shape
    nq = t // Q_TILE
    lanes = NSA_GROUP * Q_TILE
    nsp = selmap_t.shape[0]
    n_top = min(SEL_TOP, -(-t // SEL_BLOCK))
    return pl.pallas_call(
        functools.partial(_nsa_cmp_p_kernel, n_top),
        grid=(nkv, nq),
        in_specs=[pl.BlockSpec((NSA_GROUP * hd, Q_TILE), lambda k, i: (k, i)),
                  pl.BlockSpec((None, ncp, hd), lambda k, i: (k, 0, 0)),
                  pl.BlockSpec((None, hd, ncp), lambda k, i: (k, 0, 0)),
                  pl.BlockSpec((nsp, ncp), lambda k, i: (0, 0)),
                  pl.BlockSpec((None, 1, lanes), lambda k, i: (k, 0, 0))],
        out_specs=[pl.BlockSpec((None, None, hd, lanes), lambda k, i: (k, i, 0, 0)),
                   pl.BlockSpec((None, nsp, Q_TILE), lambda k, i: (k, 0, i))],
        out_shape=[jax.ShapeDtypeStruct((nkv, nq, hd, lanes), F32),
                   jax.ShapeDtypeStruct((nkv, nsp, t), F32)],
        compiler_params=_cparams(("parallel", "parallel")),
        name="nsa_cmp_prompt",
    )(q_t, kc, vct, selmap_t, slopes)


QX_ROWS = 128
QX_DYN = NSA_HD
QX_STATIC = NSA_HD + 16
KX_ONES = NSA_HD + KV_TILE // SEL_BLOCK
VX_ROWS = NSA_HD + 16


def _split3(v):
    hi = v.astype(BF16).astype(F32)
    mid = (v - hi).astype(BF16).astype(F32)
    lo = (v - hi - mid).astype(BF16).astype(F32)
    return [hi, mid, lo]


def _nsa_slc_p_kernel(q_ref, kx_ref, vx_ref, kwx_ref, vwx_ref, mask_ref, oc_ref, gl_ref, qs_ref, slope_ref,
                      o_ref, qx_ref, m_ref, acc_ref, s_ref):
    i = pl.program_id(1)
    lanes = qx_ref.shape[1]
    t0 = i * Q_TILE
    tq = t0 + _iota((1, Q_TILE), 1)
    tq_f = (t0 + _iota((1, lanes), 1) % Q_TILE).astype(F32)
    slope = slope_ref[...]
    bpt = KV_TILE // SEL_BLOCK
    qx_ref[0:NSA_HD, :] = _group_queries(q_ref)
    qx_ref[QX_STATIC:QX_ROWS, :] = qs_ref[...]

    def set_tile_rows(j, mask_rows):
        off = _split3(slope * (jnp.asarray(j * KV_TILE).astype(F32) - tq_f))
        pad = jnp.zeros((QX_STATIC - QX_DYN - bpt - 3, lanes), F32)
        rows = jnp.concatenate([jnp.concatenate([mask_rows] * NSA_GROUP, axis=1)] + off + [pad], axis=0)
        qx_ref[QX_DYN:QX_STATIC, :] = rows.astype(BF16)

    width = lanes // 2
    halves = [slice(hf * width, (hf + 1) * width) for hf in range(2)]

    def values(v_ref, j, accs, alphas, ps):
        v_tile = v_ref[:, pl.ds(pl.multiple_of(j * KV_TILE, KV_TILE), KV_TILE)]
        return [a * acc + jnp.dot(v_tile, p, preferred_element_type=F32) for acc, a, p in zip(accs, alphas, ps)]

    def weights(j, allowed_fn):
        bias = None
        if allowed_fn is not None:
            bias = jnp.where(allowed_fn(j * KV_TILE + _iota((KV_TILE, 1), 0)), 0.0, NEG)
            bias = jnp.concatenate([bias] * (width // Q_TILE), axis=1)
        alphas, ps = [], []
        for hs in halves:
            s = s_ref[:, hs]
            if bias is not None:
                s = s + bias
            m_old = m_ref[:, hs]
            m_new = jnp.maximum(m_old, jnp.max(s, 0, keepdims=True))
            m_ref[:, hs] = m_new
            alphas.append(jnp.exp2(m_old - m_new))
            ps.append(jnp.exp2(s - m_new).astype(BF16))
        return alphas, ps

    def scores(k_ref, j, tile_mask_rows):
        set_tile_rows(j, tile_mask_rows)
        k_tile = k_ref[pl.ds(pl.multiple_of(j * KV_TILE, KV_TILE), KV_TILE), :]
        return [jnp.dot(k_tile, qx_ref[:, hs], preferred_element_type=F32) for hs in halves]

    def branch(k_ref, v_ref, first, last, mask_rows_fn, body_allowed, last_allowed):
        m_ref[...] = jnp.full(m_ref.shape, NEG, F32)
        acc_ref[...] = jnp.zeros(acc_ref.shape, F32)

        def one(j, allowed_fn):
            for hs, s in zip(halves, scores(k_ref, j, mask_rows_fn(j))):
                s_ref[:, hs] = s
            alphas, ps = weights(j, allowed_fn)
            accs = values(v_ref, j, [acc_ref[:, hs] for hs in halves], alphas, ps)
            for hs, acc in zip(halves, accs):
                acc_ref[:, hs] = acc

        def body(j, carry):
            one(j, body_allowed)
            return carry

        lax.fori_loop(first, last, body, 0)
        one(last, last_allowed)
        return acc_ref[0:NSA_HD, :] / acc_ref[NSA_HD:NSA_HD + 1, :]

    causal = lambda spos: spos <= tq
    in_window = lambda spos: (spos <= tq) & (spos > tq - WINDOW)

    def mask_bias(j):
        rows = mask_ref[pl.ds(pl.multiple_of(j * bpt, bpt), bpt), :]
        return jnp.where(rows > 0.5, 0.0, NEG)

    o_s = branch(kx_ref, vx_ref, 0, (t0 + Q_TILE + KV_TILE - 1) // KV_TILE - 1, mask_bias, None, causal)
    j_hi = t0 // KV_TILE
    o_w = branch(kwx_ref, vwx_ref, jnp.maximum(j_hi - WINDOW // KV_TILE, 0), j_hi,
                 lambda j: jnp.zeros((bpt, Q_TILE), F32), in_window, in_window)

    gates = jax.nn.sigmoid(gl_ref[...])
    o = gates[0:1, :] * oc_ref[...] + gates[1:2, :] * o_s + gates[2:3, :] * o_w
    for g in range(NSA_GROUP):
        o_ref[g * NSA_HD:(g + 1) * NSA_HD, :] = o[:, g * Q_TILE:(g + 1) * Q_TILE]


def nsa_slc_prompt(q_t, kx, vx, kwx, vwx, mask_t, oc_t, gl_t, qs, slopes):
    nkv, t, _ = kx.shape
    hd = NSA_HD
    nq = t // Q_TILE
    lanes = NSA_GROUP * Q_TILE
    nsp = mask_t.shape[1]
    return pl.pallas_call(
        _nsa_slc_p_kernel,
        grid=(nkv, nq),
        in_specs=[pl.BlockSpec((NSA_GROUP * hd, Q_TILE), lambda k, i: (k, i)),
                  pl.BlockSpec((None, t, QX_ROWS), lambda k, i: (k, 0, 0)),
                  pl.BlockSpec((None, VX_ROWS, t), lambda k, i: (k, 0, 0)),
                  pl.BlockSpec((None, t, QX_ROWS), lambda k, i: (k, 0, 0)),
                  pl.BlockSpec((None, VX_ROWS, t), lambda k, i: (k, 0, 0)),
                  pl.BlockSpec((None, nsp, Q_TILE), lambda k, i: (k, 0, i)),
                  pl.BlockSpec((None, None, hd, lanes), lambda k, i: (k, i, 0, 0)),
                  pl.BlockSpec((None, None, 3, lanes), lambda k, i: (k, i, 0, 0)),
                  pl.BlockSpec((None, QX_ROWS - QX_STATIC, lanes), lambda k, i: (k, 0, 0)),
                  pl.BlockSpec((None, 1, lanes), lambda k, i: (k, 0, 0))],
        out_specs=pl.BlockSpec((NSA_GROUP * hd, Q_TILE), lambda k, i: (k, i)),
        out_shape=jax.ShapeDtypeStruct((nkv * NSA_GROUP * hd, t), F32),
        scratch_shapes=[pltpu.VMEM((QX_ROWS, lanes), BF16),
                        pltpu.VMEM((1, lanes), F32),
                        pltpu.VMEM((VX_ROWS, lanes), F32),
                        pltpu.VMEM((KV_TILE, lanes), F32)],
        compiler_params=_cparams(("parallel", "arbitrary")),
        name="nsa_slc_prompt",
    )(q_t, kx, vx, kwx, vwx, mask_t, oc_t, gl_t, qs, slopes)


def _diag_blocks(o_full, rows_per_head):
    return jnp.concatenate(
        [o_full[k * rows_per_head:(k + 1) * rows_per_head, k * NSA_HD:(k + 1) * NSA_HD]
         for k in range(NSA_KV_HEADS)], axis=0)


def _nsa_cmp_s_kernel(n_top, qbd_ref, kc_ref, vc_ref, selmap_ref, gsum_ref, slope_ref, tpos_ref,
                      oc_ref, mask_ref):
    tpos = tpos_ref[...]
    p = _cmp_softmax(_mm(kc_ref[...], qbd_ref[...]), slope_ref[...], tpos)
    lanes = p.shape[1]
    oc_ref[...] = _diag_blocks(_mm_tn(p, vc_ref[...]), lanes // NSA_KV_HEADS)
    imp = _mm_split_l(_mm_split(selmap_ref[...], p), gsum_ref[...])
    mask_ref[...] = _select_blocks(imp, tpos, n_top)


def nsa_cmp_sample(qbd, cmp_kv, selmap_t, gsum, slope_row, tpos_row, ls):
    b, width, lanes = qbd.shape
    ncp = cmp_kv.shape[1]
    nsp = selmap_t.shape[0]
    n_top = min(SEL_TOP, -(-ls // SEL_BLOCK))
    return pl.pallas_call(
        functools.partial(_nsa_cmp_s_kernel, n_top),
        grid=(b,),
        in_specs=[pl.BlockSpec((None, width, lanes), lambda bi: (bi, 0, 0)),
                  pl.BlockSpec((None, ncp, width), lambda bi: (bi, 0, 0)),
                  pl.BlockSpec((None, ncp, width), lambda bi: (bi, 0, 1)),
                  pl.BlockSpec((nsp, ncp), lambda bi: (0, 0)),
                  pl.BlockSpec((lanes, lanes), lambda bi: (0, 0)),
                  pl.BlockSpec((1, lanes), lambda bi: (0, 0)),
                  pl.BlockSpec((1, lanes), lambda bi: (0, 0))],
        out_specs=[pl.BlockSpec((None, lanes, NSA_HD), lambda bi: (bi, 0, 0)),
                   pl.BlockSpec((None, nsp, lanes), lambda bi: (bi, 0, 0))],
        out_shape=[jax.ShapeDtypeStruct((b, lanes, NSA_HD), F32),
                   jax.ShapeDtypeStruct((b, nsp, lanes), F32)],
        compiler_params=_cparams(("parallel",)),
        name="nsa_cmp_sample",
    )(qbd, cmp_kv, cmp_kv, selmap_t, gsum, slope_row, tpos_row)


def _nsa_slc_s_kernel(npg, past, w_start, pt_ref, *refs):
    pages = refs[:npg]
    (qbd_ref, mask_ref, new_ref, win_ref, oc_ref, gl_ref, slope_ref, tpos_ref,
     o_ref, m_ref, l_ref, acc_ref) = refs[npg:]
    s = pl.program_id(1)
    ns = pl.num_programs(1)
    qbd = qbd_ref[...]
    width = qbd.shape[0]
    lanes = qbd.shape[1]
    tpos = tpos_ref[...]
    tf = tpos.astype(F32)
    slope = slope_ref[...]
    page = pages[0].shape[0]
    bpp = page // SEL_BLOCK

    @pl.when(s == 0)
    def _init():
        m_ref[...] = jnp.full(m_ref.shape, NEG, F32)
        l_ref[...] = jnp.zeros(l_ref.shape, F32)
        acc_ref[...] = jnp.zeros(acc_ref.shape, F32)

    def scores(k_rows, pos0):
        n = k_rows.shape[0]
        spos = pos0 + _iota((n, 1), 0)
        return _mm(k_rows, qbd) - slope * (tf - spos.astype(F32)), spos

    def update(sc, v_rows):
        alpha, p = _online_update(sc, m_ref, l_ref)
        acc_ref[...] = _row_to_col(alpha) * acc_ref[...] + _mm_tn(p, v_rows)

    k_rows = jnp.concatenate([pg[:, 0:width] for pg in pages], axis=0)
    v_rows = jnp.concatenate([pg[:, width:2 * width] for pg in pages], axis=0)
    sc, spos = scores(k_rows, s * (npg * page))
    nblk = npg * bpp
    rows = mask_ref[pl.ds(pl.multiple_of(s * nblk, nblk), nblk), :]
    allowed = (_expand_rows(rows, SEL_BLOCK) > 0.5) & (spos <= tpos)
    update(jnp.where(allowed, sc, NEG), v_rows)

    @pl.when(s == ns - 1)
    def _finish():
        new = new_ref[...]
        sc_n, spos_n = scores(new[:, 0:width], past)
        blk0 = past // SEL_BLOCK
        row = mask_ref[pl.ds(blk0, 8), :][0:1, :]
        ok_n = (row > 0.5) & (spos_n <= tpos)
        update(jnp.where(ok_n, sc_n, NEG), new[:, width:2 * width])
        o_s = acc_ref[...] / _row_to_col(l_ref[...])
        win = win_ref[...]
        sc_w, wpos = scores(win[:, 0:width], w_start)
        ok_w = (wpos >= 0) & (wpos <= tpos) & (wpos > tpos - WINDOW)
        sc_w = jnp.where(ok_w, sc_w, NEG)
        p_w = jnp.exp(sc_w - jnp.max(sc_w, 0, keepdims=True))
        o_w = _mm_tn(p_w, win[:, width:2 * width]) / _row_to_col(jnp.sum(p_w, 0, keepdims=True))
        rph = lanes // NSA_KV_HEADS
        gates = jax.nn.sigmoid(gl_ref[...])
        o_ref[...] = (gates[:, 0:1] * oc_ref[...] + gates[:, 1:2] * _diag_blocks(o_s, rph)
                      + gates[:, 2:3] * _diag_blocks(o_w, rph))


def nsa_slc_sample(pages, page_table, qbd, mask_t, new_rows, win_all, oc, gl, slope_row, tpos_row,
                   past, w_start):
    b, n_pages = page_table.shape
    npg = SLC_PAGES
    page, width2 = pages.shape[1:]
    width, lanes = qbd.shape[1:]
    nsp = mask_t.shape[1]
    n_new = new_rows.shape[1]
    n_win = win_all.shape[1]
    page_specs = [pl.BlockSpec((None, page, width2), functools.partial(lambda r, bi, s, pt: (pt[bi, s * npg + r], 0, 0), r))
                  for r in range(npg)]
    per_b = lambda shape: pl.BlockSpec((None,) + shape, lambda bi, s, pt: (bi, 0, 0))
    const = lambda shape: pl.BlockSpec(shape, lambda bi, s, pt: (0, 0))
    grid_spec = pltpu.PrefetchScalarGridSpec(
        num_scalar_prefetch=1,
        grid=(b, n_pages // npg),
        in_specs=page_specs + [per_b((width, lanes)), per_b((nsp, lanes)), per_b((n_new, width2)),
                               per_b((n_win, width2)), per_b((lanes, NSA_HD)), per_b((lanes, 8)),
                               const((1, lanes)), const((1, lanes))],
        out_specs=per_b((lanes, NSA_HD)),
        scratch_shapes=[pltpu.VMEM((1, lanes), F32),
                        pltpu.VMEM((1, lanes), F32),
                        pltpu.VMEM((lanes, width), F32)],
    )
    return pl.pallas_call(
        functools.partial(_nsa_slc_s_kernel, npg, past, w_start),
        grid_spec=grid_spec,
        out_shape=jax.ShapeDtypeStruct((b, lanes, NSA_HD), F32),
        compiler_params=_cparams(("parallel", "arbitrary")),
        name="nsa_slc_sample",
    )(page_table, *([pages] * npg), qbd, mask_t, new_rows, win_all, oc, gl, slope_row, tpos_row)


def _alibi_slopes():
    return 2.0 ** (-8.0 * (np.arange(NSA_HEADS) + 1) / NSA_HEADS)


def _selmap_t(nc, ns, ns_pad, ncp):
    r = CMP_BLOCK // CMP_STRIDE
    rs = SEL_BLOCK // CMP_STRIDE
    d = np.arange(nc)[:, None] - rs * np.arange(ns)[None, :]
    m = sum(((d + n >= 0) & (d + n < rs)).astype(np.float32) for n in range(r))
    out = np.zeros((ns_pad, ncp), np.float32)
    out[:ns, 1:nc + 1] = m.T
    return jnp.asarray(out, BF16)


def _mod_rows(chunk, n_sample_rep):
    top = jnp.broadcast_to(chunk[0:1], (ROW_TILE, chunk.shape[1]))
    bottom = jnp.repeat(chunk[1:], n_sample_rep, axis=0)
    return jnp.concatenate([top, bottom], 0)


def _key_extension(t):
    pos = np.arange(t)
    bpt = KV_TILE // SEL_BLOCK
    ext = np.zeros((t, QX_ROWS - NSA_HD), np.float32)
    ext[pos, (pos // SEL_BLOCK) % bpt] = 1.0
    ext[:, bpt:bpt + 3] = 1.0
    ext[:, 16:19] = (pos % LANES)[:, None]
    ext[:, 19:22] = ((pos // LANES) % (KV_TILE // LANES))[:, None]
    return jnp.asarray(ext, BF16)


def nsa_prompt(q_t, gl_t_rows, kv, cmp_kv, t):
    nkv, grp, hd = NSA_KV_HEADS, NSA_GROUP, NSA_HD
    nq = t // Q_TILE
    lanes = grp * Q_TILE
    half = nkv * hd
    log2e = float(np.log2(np.e))
    gl_t = jnp.transpose(gl_t_rows.reshape(3, nkv, grp, nq, Q_TILE), (1, 3, 0, 2, 4)).reshape(nkv, nq, 3, lanes)
    ext = jnp.broadcast_to(_key_extension(t), (nkv, t, QX_ROWS - hd))
    ones_rows = jnp.zeros((nkv, VX_ROWS - hd, t), BF16).at[:, 0, :].set(1.0)
    keys = lambda x: jnp.concatenate([jnp.transpose(x.reshape(t, nkv, hd), (1, 0, 2)).astype(BF16), ext], 2)
    vals = lambda x: jnp.concatenate([jnp.transpose(x.reshape(t, nkv, hd), (1, 2, 0)).astype(BF16), ones_rows], 1)
    kx, vx = keys(kv[:, 2 * half:3 * half]), vals(kv[:, 3 * half:4 * half])
    kwx, vwx = keys(kv[:, 4 * half:5 * half]), vals(kv[:, 5 * half:6 * half])
    ncp = cmp_kv.shape[1]
    kc = jnp.transpose(cmp_kv[0, :, 0:half].reshape(ncp, nkv, hd), (1, 0, 2)).astype(BF16)
    vct = jnp.transpose(cmp_kv[0, :, half:2 * half].reshape(ncp, nkv, hd), (1, 2, 0)).astype(BF16)
    ns = -(-t // SEL_BLOCK)
    nsp = -(-ns // SUBLANES) * SUBLANES
    selmap_t = _selmap_t(ncp - 1, ns, nsp, ncp)
    slopes = jnp.asarray(np.repeat(_alibi_slopes().reshape(nkv, 1, grp, 1), Q_TILE, axis=3).reshape(nkv, 1, lanes)
                         * log2e, F32)
    qs = jnp.concatenate(_split3(slopes) + _split3(slopes * LANES)
                         + [jnp.zeros((nkv, QX_ROWS - QX_STATIC - 6, lanes), F32)], 1).astype(BF16)
    oc_t, mask_t = nsa_cmp_prompt(q_t, kc, vct, selmap_t, slopes, t)
    return nsa_slc_prompt(q_t, kx, vx, kwx, vwx, mask_t, oc_t, gl_t, qs, slopes)


def nsa_sample(q, gl, kv_new, cmp_kv, slc_pages, page_table, win_all, b, l, past, w_start):
    nkv, grp, hd = NSA_KV_HEADS, NSA_GROUP, NSA_HD
    lanes = nkv * grp * l
    half = nkv * hd
    q5 = q.reshape(b, l, nkv, grp, hd)
    qbd = jnp.einsum('btkgd,kj->bjdkgt', q5, jnp.eye(nkv, dtype=F32)).reshape(b, half, lanes).astype(BF16)
    gl_r = jnp.transpose(gl.reshape(b, l, 3, nkv, grp), (0, 3, 4, 1, 2)).reshape(b, lanes, 3)
    gl_r = jnp.pad(gl_r, ((0, 0), (0, 0), (0, 5)))
    slope_row = jnp.asarray(np.repeat(_alibi_slopes(), l).reshape(1, lanes), F32)
    tpos_row = jnp.asarray(np.tile(past + np.arange(l), nkv * grp).reshape(1, lanes), I32)
    ls = past + l
    ncp = cmp_kv.shape[1]
    ns = -(-ls // SEL_BLOCK)
    nsp = -(-(ns + SUBLANES) // SUBLANES) * SUBLANES
    selmap_t = _selmap_t(ncp - 1, ns, nsp, ncp)
    lane = np.arange(lanes)
    same = (lane[:, None] // (grp * l) == lane[None, :] // (grp * l)) & (lane[:, None] % l == lane[None, :] % l)
    gsum = jnp.asarray(same.astype(np.float32), BF16)
    oc, mask_t = nsa_cmp_sample(qbd, cmp_kv, selmap_t, gsum, slope_row, tpos_row, ls)
    new_rows = kv_new[:, 2 * half:4 * half].reshape(b, l, 2 * half)
    o = nsa_slc_sample(slc_pages, page_table, qbd, mask_t, new_rows, win_all, oc, gl_r, slope_row, tpos_row,
                       past, w_start)
    return jnp.transpose(o.reshape(b, nkv, grp, l, hd), (0, 3, 1, 2, 4)).reshape(b * l, nkv * grp * hd)


def kernel(x_prompt, x_sample, state_gdn, state_conv, cache_cmp_kv, cache_slc_kv, cache_win_kv, page_table,
           c_prompt, c_sample, ada_w, ada_b, norm_mix, norm_ffn, gdn_w_in, gdn_conv_w, gdn_a_log, gdn_dt_bias,
           gdn_norm, gdn_w_out, kv_ada_w, kv_ada_b, kv_norm, kv_w, cmp_pe, cmp_w1, cmp_w2, nsa_w_in, nsa_w_out,
           router_w, router_b, moe_w_gu, moe_b_gu, moe_w_dn, moe_b_dn, norm_f):
    bp, seq, d = x_prompt.shape
    db, dl, _ = x_sample.shape
    assert bp == 1 and db * dl == ROW_TILE and seq % ROW_TILE == 0 and ada_w.shape[0] == 2
    n_pt = seq // ROW_TILE
    page = cache_cmp_kv.shape[1]
    past = page_table.shape[1] * page
    w_buf = cache_win_kv.shape[1]
    w_start = past - w_buf
    nkv, hd = NSA_KV_HEADS, NSA_HD
    half = nkv * hd
    kvw = 2 * half
    hw = GDN_HEADS * GDN_DK
    qkvw = 3 * hw

    c_all = jnp.concatenate([c_prompt, c_sample], 0)
    n_c = c_all.shape[0]
    c_pad = jnp.pad(c_all, ((0, -n_c % SUBLANES), (0, 0)))
    rows6 = lambda mod: [_mod_rows(ch, dl) for ch in jnp.split(mod[:n_c], mod.shape[1] // d, -1)]
    mod0 = rows6(cond_matmul(c_pad, ada_w[0], ada_b[0]))
    mod1 = rows6(cond_matmul(c_pad, ada_w[1], ada_b[1]))
    kv_sh, kv_sc = rows6(cond_matmul(c_pad, kv_ada_w, kv_ada_b))

    h = jnp.concatenate([x_prompt.reshape(seq, d), x_sample.reshape(db * dl, d)], 0)

    sh1, sc1, g1, sh2, sc2, g2 = mod0
    w_in = gdn_w_in[0]
    w_in = jnp.pad(w_in, ((0, 0), (0, -w_in.shape[1] % LANES))).astype(BF16)
    (proj,) = norm_proj(h, [(norm_mix[0], sh1, sc1, w_in, False)], n_pt)
    zeros_s = jnp.zeros((1,) + state_gdn.shape[2:], F32)
    zeros_c = jnp.zeros((1, 8, qkvw), F32)
    conv_s = jnp.pad(state_conv[0], ((0, 0), (8 - (GDN_CONV - 1), 0), (0, 0)))
    gdn_args = (gdn_conv_w[0], gdn_a_log[0], gdn_dt_bias[0], gdn_norm[0])
    o_p, p_gdn, p_conv = gdn(proj, 0, 1, seq, min(GDN_CHUNK, seq), zeros_s, zeros_c, *gdn_args)
    o_s, s_gdn, s_conv = gdn(proj, seq, db, dl, min(GDN_CHUNK, dl), state_gdn[0], conv_s, *gdn_args)
    h = proj_residual(jnp.concatenate([o_p, o_s], 0), gdn_w_out[0].astype(BF16), h, g1, n_pt)

    def expert_weights(layer):
        w_dn = moe_w_dn[layer]
        wd = jnp.stack([w_dn, jnp.zeros_like(w_dn)], 2).reshape(w_dn.shape[0], 2 * w_dn.shape[1], w_dn.shape[2])
        return (moe_w_gu[layer].astype(BF16), moe_b_gu[layer][:, None, :], wd.astype(BF16), moe_b_dn[layer][:, None, :])

    h = moe_layer(h, norm_ffn[0], sh2, sc2, g2, router_w[0], router_b[0], *expert_weights(0), n_pt)[0]

    sh1, sc1, g1, sh2, sc2, g2 = mod1
    nq_cols = NSA_HEADS * hd
    log2e = float(np.log2(np.e))
    w_q = nsa_w_in[0].T
    w_q = jnp.concatenate([w_q[:nq_cols] * (hd ** -0.5 * log2e), w_q[nq_cols:]], 0)
    w_q = jnp.pad(w_q, ((0, -w_q.shape[0] % LANES), (0, 0))).astype(BF16)
    kvp, q_t = norm_proj(h, [(kv_norm, kv_sh, kv_sc, kv_w.astype(BF16), False),
                             (norm_mix[1], sh1, sc1, w_q, True)], n_pt)
    gl_t = q_t[nq_cols:nq_cols + 3 * NSA_HEADS]
    kv_p, kv_s = kvp[:seq], kvp[seq:]
    p_cmp, p_slc, p_win = kv_p[:, 0:kvw], kv_p[:, kvw:2 * kvw], kv_p[:, 2 * kvw:3 * kvw]
    s_cmp, s_slc, s_win_new = kv_s[:, 0:kvw], kv_s[:, kvw:2 * kvw], kv_s[:, 2 * kvw:3 * kvw]

    pe_t = jnp.tile(cmp_pe, (1, 1, nkv))
    w1_bd = _block_diag4(cmp_w1).astype(BF16)
    w2_bd = _block_diag4(cmp_w2).astype(BF16)
    cmp_p = compress(p_cmp.reshape(seq // page, page, kvw), jnp.arange(seq // page, dtype=I32).reshape(1, -1),
                     pe_t, w1_bd, w2_bd)
    cmp_s = compress(cache_cmp_kv.reshape(-1, page, kvw), page_table, pe_t, w1_bd, w2_bd)

    win_all = jnp.concatenate([cache_win_kv.reshape(db, w_buf, kvw), s_win_new.reshape(db, dl, kvw)], 1)
    o_p_t = nsa_prompt(q_t, gl_t[:, :seq], kv_p, cmp_p, seq)
    q_s = q_t[:nq_cols, seq:].T * (1.0 / log2e)
    o_s = nsa_sample(q_s, gl_t[:, seq:].T, kv_s, cmp_s, cache_slc_kv.reshape(-1, page, kvw), page_table,
                     win_all, db, dl, past, w_start)
    h = proj_residual_t(o_p_t, o_s.T, nsa_w_out[0].astype(BF16), h, g1, n_pt)
    h, y = moe_layer(h, norm_ffn[1], sh2, sc2, g2, router_w[1], router_b[1], *expert_weights(1), n_pt, norm_f)

    kv5 = lambda x, b_: x.reshape(b_, -1, 2, nkv, hd)
    win_keep = lambda x: x[:, max(0, x.shape[1] - WINDOW):]
    return (y[:seq].reshape(bp, seq, d), y[seq:].reshape(db, dl, d),
            p_gdn[None], p_conv[None],
            kv5(p_cmp, bp), kv5(p_slc, bp), win_keep(kv5(p_win, bp)),
            s_gdn[None], s_conv[None],
            kv5(s_cmp, db), kv5(s_slc, db), win_keep(kv5(win_all, db)))
```

```python
import functools

import jax
import jax.numpy as jnp
import numpy as np
from jax import lax
from jax.experimental import pallas as pl
from jax.experimental.pallas import tpu as pltpu

F32 = jnp.float32
BF16 = jnp.bfloat16
I32 = jnp.int32

GDN_HEADS = 8
GDN_DK = 128
GDN_DV = 128
GDN_CONV = 4
GDN_CHUNK = 64
NSA_HEADS = 16
NSA_KV_HEADS = 4
NSA_GROUP = NSA_HEADS // NSA_KV_HEADS
NSA_HD = 64
CMP_BLOCK = 32
CMP_STRIDE = 16
SEL_BLOCK = 64
SEL_TOP = 16
SEL_FORCE = 1000.0
WINDOW = 512
TOP_K = 4
SWIGLU_LIMIT = 7.0
SWIGLU_ALPHA = 1.702
EPS = 1e-6
NEG = -1e30

LANES = 128
SUBLANES = 8
ROW_TILE = 256
MOE_TILE = 256
KV_TILE = 512
Q_TILE = 128
CMP_PAGES = 16
CMP_ROW_CHUNK = 256
SLC_PAGES = 8
VMEM_LIMIT = 56 * 1024 * 1024
HIGHEST = lax.Precision.HIGHEST


def _cparams(sem):
    return pltpu.CompilerParams(dimension_semantics=sem, vmem_limit_bytes=VMEM_LIMIT)


def _mm(a, b):
    return jnp.dot(a.astype(BF16), b.astype(BF16), preferred_element_type=F32)


def _mm_nt(a, b):
    return lax.dot_general(a.astype(BF16), b.astype(BF16), (((1,), (1,)), ((), ())),
                           preferred_element_type=F32)


def _mm_tn(a, b):
    return lax.dot_general(a.astype(BF16), b.astype(BF16), (((0,), (0,)), ((), ())),
                           preferred_element_type=F32)


def _mm32(a, b):
    return jnp.dot(a, b, precision=HIGHEST, preferred_element_type=F32)


def _mm_split(a_exact_bf16, p):
    p_hi = p.astype(BF16)
    p_lo = (p - p_hi.astype(F32)).astype(BF16)
    return (jnp.dot(a_exact_bf16, p_hi, preferred_element_type=F32)
            + jnp.dot(a_exact_bf16, p_lo, preferred_element_type=F32))


def _mm_split_l(p, b_exact_bf16):
    p_hi = p.astype(BF16)
    p_lo = (p - p_hi.astype(F32)).astype(BF16)
    return (jnp.dot(p_hi, b_exact_bf16, preferred_element_type=F32)
            + jnp.dot(p_lo, b_exact_bf16, preferred_element_type=F32))


def _row_to_col(row):
    n = row.shape[1]
    eye = _iota((n, n), 0) == _iota((n, n), 1)
    return jnp.sum(jnp.where(eye, jnp.broadcast_to(row, (n, n)), 0.0), axis=1, keepdims=True)


def _silu(x):
    return x * jax.nn.sigmoid(x)


def _iota(shape, dim):
    return lax.broadcasted_iota(I32, shape, dim)


def _cond_kernel(c_ref, w_ref, b_ref, o_ref):
    o_ref[...] = _mm(_silu(c_ref[...]), w_ref[...]) + b_ref[...]


def cond_matmul(c, w, b, tn=1024):
    m, d = c.shape
    n = w.shape[1]
    return pl.pallas_call(
        _cond_kernel,
        grid=(n // tn,),
        in_specs=[pl.BlockSpec((m, d), lambda j: (0, 0)),
                  pl.BlockSpec((d, tn), lambda j: (0, j)),
                  pl.BlockSpec((1, tn), lambda j: (0, j))],
        out_specs=pl.BlockSpec((m, tn), lambda j: (0, j)),
        out_shape=jax.ShapeDtypeStruct((m, n), F32),
        compiler_params=_cparams(("arbitrary",)),
        name="cond_matmul",
    )(c, w, b.reshape(1, n))


def _mod_map(n_prompt_tiles):
    return lambda i: (jnp.where(i < n_prompt_tiles, 0, 1), 0)


def _norm_proj_kernel(transposed, x_ref, *refs):
    n_heads = len(transposed)
    x = x_ref[...]
    xn = x * lax.rsqrt(jnp.mean(x * x, -1, keepdims=True) + EPS)
    for i in range(n_heads):
        g_ref, sh_ref, sc_ref, w_ref = refs[4 * i:4 * i + 4]
        o_ref = refs[4 * n_heads + i]
        u = (xn * g_ref[...]) * (1.0 + sc_ref[...]) + sh_ref[...]
        o_ref[...] = _mm_nt(w_ref[...], u) if transposed[i] else _mm(u, w_ref[...])


def norm_proj(h, heads, n_prompt_tiles):
    t, d = h.shape
    tm = ROW_TILE
    in_specs = [pl.BlockSpec((tm, d), lambda i: (i, 0))]
    args = [h]
    out_specs, out_shapes = [], []
    for gamma, sh, sc, w, transposed in heads:
        in_specs += [pl.BlockSpec((1, d), lambda i: (0, 0)),
                     pl.BlockSpec((tm, d), _mod_map(n_prompt_tiles)),
                     pl.BlockSpec((tm, d), _mod_map(n_prompt_tiles)),
                     pl.BlockSpec(w.shape, lambda i: (0, 0))]
        args += [gamma.reshape(1, d), sh, sc, w]
        if transposed:
            out_specs.append(pl.BlockSpec((w.shape[0], tm), lambda i: (0, i)))
            out_shapes.append(jax.ShapeDtypeStruct((w.shape[0], t), F32))
        else:
            out_specs.append(pl.BlockSpec((tm, w.shape[1]), lambda i: (i, 0)))
            out_shapes.append(jax.ShapeDtypeStruct((t, w.shape[1]), F32))
    return pl.pallas_call(
        functools.partial(_norm_proj_kernel, tuple(hd[4] for hd in heads)),
        grid=(t // tm,),
        in_specs=in_specs,
        out_specs=out_specs,
        out_shape=out_shapes,
        compiler_params=_cparams(("parallel",)),
        name="norm_proj",
    )(*args)


def _proj_res_t_kernel(n_prompt_tiles, ap_ref, as_ref, w_ref, h_ref, g_ref, o_ref):
    i = pl.program_id(0)

    @pl.when(i < n_prompt_tiles)
    def _prompt():
        o_ref[...] = h_ref[...] + g_ref[...] * _mm_tn(ap_ref[...], w_ref[...])

    @pl.when(i >= n_prompt_tiles)
    def _sample():
        o_ref[...] = h_ref[...] + g_ref[...] * _mm_tn(as_ref[...], w_ref[...])


def proj_residual_t(a_prompt_t, a_sample_t, w, h, gate_rows, n_prompt_tiles):
    t, d = h.shape
    k = w.shape[0]
    tm = ROW_TILE
    return pl.pallas_call(
        functools.partial(_proj_res_t_kernel, n_prompt_tiles),
        grid=(t // tm,),
        in_specs=[pl.BlockSpec((k, tm), lambda i: (0, jnp.minimum(i, n_prompt_tiles - 1))),
                  pl.BlockSpec((k, tm), lambda i: (0, 0)),
                  pl.BlockSpec((k, d), lambda i: (0, 0)),
                  pl.BlockSpec((tm, d), lambda i: (i, 0)),
                  pl.BlockSpec((tm, d), _mod_map(n_prompt_tiles))],
        out_specs=pl.BlockSpec((tm, d), lambda i: (i, 0)),
        out_shape=jax.ShapeDtypeStruct((t, d), F32),
        compiler_params=_cparams(("parallel",)),
        name="proj_residual_t",
    )(a_prompt_t, a_sample_t, w, h, gate_rows)


def _proj_res_kernel(a_ref, w_ref, h_ref, g_ref, o_ref):
    o_ref[...] = h_ref[...] + g_ref[...] * _mm(a_ref[...], w_ref[...])


def proj_residual(a, w, h, gate_rows, n_prompt_tiles):
    t, k = a.shape
    d = w.shape[1]
    tm = ROW_TILE
    return pl.pallas_call(
        _proj_res_kernel,
        grid=(t // tm,),
        in_specs=[pl.BlockSpec((tm, k), lambda i: (i, 0)),
                  pl.BlockSpec((k, d), lambda i: (0, 0)),
                  pl.BlockSpec((tm, d), lambda i: (i, 0)),
                  pl.BlockSpec((tm, d), _mod_map(n_prompt_tiles))],
        out_specs=pl.BlockSpec((tm, d), lambda i: (i, 0)),
        out_shape=jax.ShapeDtypeStruct((t, d), F32),
        compiler_params=_cparams(("parallel",)),
        name="proj_residual",
    )(a, w, h, gate_rows)


def _mm3(a, b):
    a_hi = a.astype(BF16)
    b_hi = b.astype(BF16)
    a_lo = (a - a_hi.astype(F32)).astype(BF16)
    b_lo = (b - b_hi.astype(F32)).astype(BF16)
    dot = lambda x, y: jnp.dot(x, y, preferred_element_type=F32)
    return dot(a_hi, b_hi) + (dot(a_hi, b_lo) + dot(a_lo, b_hi))


def _tri_inv(ms, c):
    r = _iota((c, c), 0)
    col = _iota((c, c), 1)
    eye = (r == col).astype(F32)
    same8 = (r // 8) == (col // 8)
    ds = [jnp.where(same8, m, 0.0) for m in ms]
    d2s = [_mm3(d, d) for d in ds]
    xs = [eye - d for d in ds]
    d4s = [_mm3(d2, d2) for d2 in d2s]
    xs = [x + _mm3(x, d2) for x, d2 in zip(xs, d2s)]
    xs = [x + _mm3(x, d4) for x, d4 in zip(xs, d4s)]
    size = 8
    while size < c:
        off = ((r // (2 * size)) == (col // (2 * size))) & ((r // size) != (col // size))
        ys = [_mm3(x, jnp.where(off, m, 0.0)) for x, m in zip(xs, ms)]
        xs = [x - _mm3(y, x) for x, y in zip(xs, ys)]
        size *= 2
    return xs


def _cumsum_rows(x):
    r = _iota(x.shape, 0)
    sh = 1
    while sh < x.shape[0]:
        x = x + jnp.where(r >= sh, pltpu.roll(x, sh, 0), 0.0)
        sh *= 2
    return x


def _gdn_kernel(c, q_ref, k_ref, v_ref, z_ref, ba_ref, conv0_ref, s0_ref, cw_ref, alog_ref, dtb_ref, ng_ref,
                o_ref, sfin_ref, convn_ref, xs_ref, xc_ref, st_ref):
    n = pl.program_id(1)
    nh, dk = GDN_HEADS, GDN_DK
    hw = nh * dk

    @pl.when(n == 0)
    def _init():
        xs_ref[0:8, :] = conv0_ref[...]
        st_ref[...] = s0_ref[...]

    xs_ref[8:8 + c, 0:hw] = q_ref[...]
    xs_ref[8:8 + c, hw:2 * hw] = k_ref[...]
    xs_ref[8:8 + c, 2 * hw:3 * hw] = v_ref[...]
    acc = cw_ref[3:4, :] * xs_ref[8:8 + c, :]
    for j in range(GDN_CONV - 1):
        acc = acc + cw_ref[j:j + 1, :] * xs_ref[5 + j:5 + j + c, :]
    xc_ref[...] = _silu(acc)
    convn_ref[...] = xs_ref[5 + c:8 + c, :]
    xs_ref[0:8, :] = xs_ref[c:c + 8, :]

    ba = ba_ref[...]
    beta_all = jax.nn.sigmoid(ba)
    xa = ba + dtb_ref[...]
    softplus = jnp.maximum(xa, 0.0) + jnp.log(1.0 + jnp.exp(-jnp.abs(xa)))
    g_all = -jnp.exp(alog_ref[...]) * softplus

    r = _iota((c, c), 0)
    col = _iota((c, c), 1)
    incl = r >= col
    strict = r > col
    eye = r == col
    heads = range(nh)
    qh = [xc_ref[:, h * dk:(h + 1) * dk] for h in heads]
    kh = [xc_ref[:, hw + h * dk:hw + (h + 1) * dk] for h in heads]
    vh = [xc_ref[:, 2 * hw + h * dk:2 * hw + (h + 1) * dk] for h in heads]
    qn = [x * lax.rsqrt(jnp.sum(x * x, -1, keepdims=True) + EPS) * (dk ** -0.5) for x in qh]
    kn = [x * lax.rsqrt(jnp.sum(x * x, -1, keepdims=True) + EPS) for x in kh]
    beta = [beta_all[:, h:h + 1] for h in heads]
    gc_b = [_cumsum_rows(jnp.broadcast_to(g_all[:, nh + h:nh + h + 1], (c, dk))) for h in heads]
    gc_row = [jnp.sum(jnp.where(eye, g[:, 0:c], 0.0), axis=0, keepdims=True) for g in gc_b]
    decay = [jnp.where(incl, jnp.exp(jnp.where(incl, g[:, 0:c] - gr, 0.0)), 0.0) for g, gr in zip(gc_b, gc_row)]
    kb = [k * b for k, b in zip(kn, beta)]
    vb = [v * b for v, b in zip(vh, beta)]
    kk = [_mm_nt(a, k) for a, k in zip(kb, kn)]
    qk = [_mm_nt(q, k) for q, k in zip(qn, kn)]
    t_inv = _tri_inv([jnp.where(strict, x * dcy, 0.0) for x, dcy in zip(kk, decay)], c)
    attn = [jnp.where(incl, x * dcy, 0.0) for x, dcy in zip(qk, decay)]
    egc = [jnp.exp(g) for g in gc_b]
    u = [_mm(t, v) for t, v in zip(t_inv, vb)]
    w = [_mm(t, a * e) for t, a, e in zip(t_inv, kb, egc)]
    gl_b = [g[c - 1:c, :] for g in gc_b]
    kg = [k * jnp.exp(gl - g) for k, gl, g in zip(kn, gl_b, gc_b)]
    s = [st_ref[h] for h in heads]
    ws = [_mm(a, b) for a, b in zip(w, s)]
    o1 = [_mm(q * e, b) for q, e, b in zip(qn, egc, s)]
    v_new = [a - b for a, b in zip(u, ws)]
    o2 = [_mm(a, v) for a, v in zip(attn, v_new)]
    kv = [_mm_tn(k, v) for k, v in zip(kg, v_new)]
    for h in heads:
        s_new = s[h] * jnp.exp(gl_b[h]) + kv[h]
        st_ref[h] = s_new
        sfin_ref[h] = s_new
        o = o1[h] + o2[h]
        on = o * lax.rsqrt(jnp.mean(o * o, -1, keepdims=True) + EPS) * ng_ref[...]
        o_ref[:, h * dk:(h + 1) * dk] = on * _silu(z_ref[:, h * dk:(h + 1) * dk])


def gdn(proj, row_off, b, l, c, s0, conv0, conv_w, a_log, dt_bias, norm_g):
    nh, dk = GDN_HEADS, GDN_DK
    hw = nh * dk
    nchunks = l // c
    rb0 = row_off // c
    row = lambda bi, n: rb0 + bi * nchunks + n
    gate_blk = 4 * hw // LANES
    lane_pad = jnp.zeros((LANES - 2 * nh,), F32)
    alog_row = jnp.concatenate([jnp.zeros((nh,), F32), a_log, lane_pad]).reshape(1, LANES)
    dtb_row = jnp.concatenate([jnp.zeros((nh,), F32), dt_bias, lane_pad]).reshape(1, LANES)
    return pl.pallas_call(
        functools.partial(_gdn_kernel, c),
        grid=(b, nchunks),
        in_specs=[pl.BlockSpec((c, hw), lambda bi, n: (row(bi, n), 0)),
                  pl.BlockSpec((c, hw), lambda bi, n: (row(bi, n), 1)),
                  pl.BlockSpec((c, hw), lambda bi, n: (row(bi, n), 2)),
                  pl.BlockSpec((c, hw), lambda bi, n: (row(bi, n), 3)),
                  pl.BlockSpec((c, LANES), lambda bi, n: (row(bi, n), gate_blk)),
                  pl.BlockSpec((None, 8, 3 * hw), lambda bi, n: (bi, 0, 0)),
                  pl.BlockSpec((None, nh, dk, GDN_DV), lambda bi, n: (bi, 0, 0, 0)),
                  pl.BlockSpec((GDN_CONV, 3 * hw), lambda bi, n: (0, 0)),
                  pl.BlockSpec((1, LANES), lambda bi, n: (0, 0)),
                  pl.BlockSpec((1, LANES), lambda bi, n: (0, 0)),
                  pl.BlockSpec((1, GDN_DV), lambda bi, n: (0, 0))],
        out_specs=[pl.BlockSpec((c, hw), lambda bi, n: (bi * nchunks + n, 0)),
                   pl.BlockSpec((None, nh, dk, GDN_DV), lambda bi, n: (bi, 0, 0, 0)),
                   pl.BlockSpec((None, GDN_CONV - 1, 3 * hw), lambda bi, n: (bi, 0, 0))],
        out_shape=[jax.ShapeDtypeStruct((b * l, hw), F32),
                   jax.ShapeDtypeStruct((b, nh, dk, GDN_DV), F32),
                   jax.ShapeDtypeStruct((b, GDN_CONV - 1, 3 * hw), F32)],
        scratch_shapes=[pltpu.VMEM((c + 8, 3 * hw), F32),
                        pltpu.VMEM((c, 3 * hw), F32),
                        pltpu.VMEM((nh, dk, GDN_DV), F32)],
        compiler_params=_cparams(("parallel", "arbitrary")),
        name="gdn",
    )(proj, proj, proj, proj, proj, conv0, s0, conv_w, alog_row, dtb_row, norm_g.reshape(1, GDN_DV))


def _route_kernel(x_ref, g_ref, sh_ref, sc_ref, rw_ref, rb_ref, u_ref, e_ref, w_ref):
    x = x_ref[...]
    xn = x * lax.rsqrt(jnp.mean(x * x, -1, keepdims=True) + EPS)
    u = (xn * g_ref[...]) * (1.0 + sc_ref[...]) + sh_ref[...]
    u_ref[...] = u
    logits = _mm32(u, rw_ref[...]) + rb_ref[...]
    tm, ne = logits.shape
    eidx = _iota((tm, ne), 1)
    lane = _iota((tm, LANES), 1)
    e_out = jnp.zeros((tm, LANES), I32)
    vals = []
    for k in range(TOP_K):
        m = jnp.max(logits, -1, keepdims=True)
        idx = jnp.min(jnp.where(logits == m, eidx, ne), -1, keepdims=True)
        logits = jnp.where(eidx == idx, -3e38, logits)
        vals.append(m)
        e_out = jnp.where(lane == k, idx, e_out)
    ex = [jnp.exp(v - vals[0]) for v in vals]
    den = ex[0] + ex[1] + ex[2] + ex[3]
    w_out = jnp.zeros((tm, LANES), F32)
    for k in range(TOP_K):
        w_out = jnp.where(lane == k, ex[k] / den, w_out)
    e_ref[...] = e_out
    w_ref[...] = w_out


def moe_route(h, gamma, sh, sc, router_w, router_b, n_prompt_tiles):
    t, d = h.shape
    ne = router_w.shape[1]
    tm = ROW_TILE
    return pl.pallas_call(
        _route_kernel,
        grid=(t // tm,),
        in_specs=[pl.BlockSpec((tm, d), lambda i: (i, 0)),
                  pl.BlockSpec((1, d), lambda i: (0, 0)),
                  pl.BlockSpec((tm, d), _mod_map(n_prompt_tiles)),
                  pl.BlockSpec((tm, d), _mod_map(n_prompt_tiles)),
                  pl.BlockSpec((d, ne), lambda i: (0, 0)),
                  pl.BlockSpec((1, ne), lambda i: (0, 0))],
        out_specs=[pl.BlockSpec((tm, d), lambda i: (i, 0)),
                   pl.BlockSpec((tm, LANES), lambda i: (i, 0)),
                   pl.BlockSpec((tm, LANES), lambda i: (i, 0))],
        out_shape=[jax.ShapeDtypeStruct((t, d), F32),
                   jax.ShapeDtypeStruct((t, LANES), I32),
                   jax.ShapeDtypeStruct((t, LANES), F32)],
        compiler_params=_cparams(("parallel",)),
        name="moe_route",
    )(h, gamma.reshape(1, d), sh, sc, router_w, router_b.reshape(1, ne))


def _expert_kernel(be_ref, nu_ref, tok0_ref, tokn_ref, dst_ref, x_hbm, wgu_ref, bgu_ref, wd_ref, bd_ref,
                   out_hbm, xbuf, ybuf, gsem, ssem):
    i = pl.program_id(0)
    n_used = nu_ref[0]
    slot = lax.rem(i, 2)
    tm = xbuf.shape[1]
    dump0 = out_hbm.shape[0] - 2 * tm

    def row_in(tok, s, j):
        return pltpu.make_async_copy(x_hbm.at[pl.ds(tok, 1)], xbuf.at[s, pl.ds(j, 1)], gsem.at[s])

    def row_out(dst, s, j):
        return pltpu.make_async_copy(ybuf.at[s, pl.ds(j, 1)], out_hbm.at[pl.ds(dst, 1)], ssem.at[s])

    def start_gather(tok_ref, s):
        for j in range(tm):
            row_in(tok_ref[0, j], s, j).start(priority=j % 2)

    def wait_gather(s):
        for j in range(tm):
            row_in(0, s, j).wait()

    def wait_scatter(s):
        for j in range(tm):
            row_out(0, s, j).wait()

    @pl.when(i == 0)
    def _first():
        start_gather(tok0_ref, 0)
        ybuf[...] = jnp.zeros(ybuf.shape, F32)
        for s in range(2):
            pltpu.make_async_copy(ybuf.at[s], out_hbm.at[pl.ds(dump0 + s * tm, tm)], ssem.at[s]).start()

    @pl.when(i < n_used)
    def _block():
        wait_gather(slot)
        start_gather(tokn_ref, 1 - slot)
        wait_scatter(slot)
        gu = _mm(xbuf[slot], wgu_ref[...]) + bgu_ref[...]
        parts = []
        for c in range(gu.shape[1] // LANES):
            pair = gu[:, c * LANES:(c + 1) * LANES]
            g = jnp.minimum(pair, SWIGLU_LIMIT)
            u = pltpu.roll(jnp.clip(pair, -SWIGLU_LIMIT, SWIGLU_LIMIT), LANES - 1, 1)
            parts.append((u + 1.0) * (g * jax.nn.sigmoid(SWIGLU_ALPHA * g)))
        ybuf[slot] = _mm(jnp.concatenate(parts, axis=1), wd_ref[...]) + bd_ref[...]
        for j in range(tm):
            row_out(dst_ref[0, j], slot, j).start(priority=j % 2)

        @pl.when(i == n_used - 1)
        def _drain():
            wait_gather(1 - slot)
            wait_scatter(slot)
            wait_scatter(1 - slot)


def moe_experts(u, row_tok, row_dst, blk_e, n_used, wgu, bgu, wd, bd, n_out_rows):
    t, d = u.shape
    ne, _, f2 = wgu.shape
    tm = MOE_TILE
    nb = row_tok.shape[0] // tm
    tok3 = row_tok.reshape(nb, 1, tm)
    dst3 = row_dst.reshape(nb, 1, tm)
    smem_blk = lambda imap: pl.BlockSpec((None, 1, tm), imap, memory_space=pltpu.SMEM)
    grid_spec = pltpu.PrefetchScalarGridSpec(
        num_scalar_prefetch=2,
        grid=(nb,),
        in_specs=[smem_blk(lambda i, be, nu: (i, 0, 0)),
                  smem_blk(lambda i, be, nu: (jnp.minimum(i + 1, nb - 1), 0, 0)),
                  smem_blk(lambda i, be, nu: (i, 0, 0)),
                  pl.BlockSpec(memory_space=pl.ANY),
                  pl.BlockSpec((None, d, f2), lambda i, be, nu: (be[i], 0, 0)),
                  pl.BlockSpec((None, 1, f2), lambda i, be, nu: (be[i], 0, 0)),
                  pl.BlockSpec((None, f2, d), lambda i, be, nu: (be[i], 0, 0)),
                  pl.BlockSpec((None, 1, d), lambda i, be, nu: (be[i], 0, 0))],
        out_specs=pl.BlockSpec(memory_space=pl.ANY),
        scratch_shapes=[pltpu.VMEM((2, tm, d), F32),
                        pltpu.VMEM((2, tm, d), F32),
                        pltpu.SemaphoreType.DMA((2,)),
                        pltpu.SemaphoreType.DMA((2,))],
    )
    return pl.pallas_call(
        _expert_kernel,
        grid_spec=grid_spec,
        out_shape=jax.ShapeDtypeStruct((n_out_rows, d), F32),
        compiler_params=_cparams(("arbitrary",)),
        name="moe_experts",
    )(blk_e, n_used, tok3, tok3, dst3, u, wgu, bgu, wd, bd)


def route_tables(top_e, n_exp):
    t = top_e.shape[0]
    tm = MOE_TILE
    m = t * TOP_K
    flat_e = top_e.reshape(-1)
    order = jnp.argsort(flat_e, stable=True).astype(I32)
    counts = jnp.sum((flat_e[:, None] == jnp.arange(n_exp, dtype=I32)[None, :]).astype(I32), 0)
    padded = (counts + tm - 1) // tm * tm
    start = jnp.cumsum(counts) - counts
    pend = jnp.cumsum(padded)
    pstart = pend - padded
    nb = -(-(m + n_exp * (tm - 1)) // tm)
    blk_e = jnp.minimum(jnp.searchsorted(pend, jnp.arange(nb, dtype=I32) * tm, side='right'), n_exp - 1).astype(I32)
    n_used = (pend[-1] // tm).astype(I32).reshape(1)
    per_row = lambda x: jnp.repeat(x[blk_e], tm)
    ridx = jnp.arange(nb * tm, dtype=I32)
    rank = ridx - per_row(pstart)
    valid = rank < per_row(counts)
    flat_src = order[jnp.where(valid, per_row(start) + rank, 0)]
    tok = flat_src // TOP_K
    slot = flat_src % TOP_K
    pad_dst = TOP_K * t + ((ridx // tm) % 2) * tm + ridx % tm
    row_tok = jnp.where(valid, tok, 0)
    row_dst = jnp.where(valid, slot * t + tok, pad_dst)
    return row_tok, row_dst, blk_e, n_used


def _combine_kernel(final, h_ref, g_ref, w_ref, s0_ref, s1_ref, s2_ref, s3_ref, *rest):
    w = w_ref[...]
    acc = (w[:, 0:1] * s0_ref[...] + w[:, 1:2] * s1_ref[...]
           + w[:, 2:3] * s2_ref[...] + w[:, 3:4] * s3_ref[...])
    hn = h_ref[...] + g_ref[...] * acc
    if final:
        nf_ref, o_ref, y_ref = rest
        y_ref[...] = hn * lax.rsqrt(jnp.mean(hn * hn, -1, keepdims=True) + EPS) * nf_ref[...]
    else:
        (o_ref,) = rest
    o_ref[...] = hn


def moe_combine(h, gate_rows, gate_w, slots, n_prompt_tiles, norm_f=None):
    t, d = h.shape
    tm = ROW_TILE
    nt = t // tm
    final = norm_f is not None
    in_specs = [pl.BlockSpec((tm, d), lambda i: (i, 0)),
                pl.BlockSpec((tm, d), _mod_map(n_prompt_tiles)),
                pl.BlockSpec((tm, LANES), lambda i: (i, 0))]
    in_specs += [pl.BlockSpec((tm, d), functools.partial(lambda k, i: (k * nt + i, 0), k)) for k in range(TOP_K)]
    args = [h, gate_rows, gate_w, slots, slots, slots, slots]
    out_specs = [pl.BlockSpec((tm, d), lambda i: (i, 0))]
    out_shape = [jax.ShapeDtypeStruct((t, d), F32)]
    if final:
        in_specs.append(pl.BlockSpec((1, d), lambda i: (0, 0)))
        args.append(norm_f.reshape(1, d))
        out_specs.append(pl.BlockSpec((tm, d), lambda i: (i, 0)))
        out_shape.append(jax.ShapeDtypeStruct((t, d), F32))
    return pl.pallas_call(
        functools.partial(_combine_kernel, final),
        grid=(nt,),
        in_specs=in_specs,
        out_specs=out_specs,
        out_shape=out_shape,
        compiler_params=_cparams(("parallel",)),
        name="moe_combine",
    )(*args)


def moe_layer(h, gamma, sh, sc, g2, router_w, router_b, wgu, bgu, wd, bd, n_prompt_tiles, norm_f=None):
    t = h.shape[0]
    u, top_e, gate_w = moe_route(h, gamma, sh, sc, router_w, router_b, n_prompt_tiles)
    row_tok, row_dst, blk_e, n_used = route_tables(top_e[:, :TOP_K], router_w.shape[1])
    slots = moe_experts(u, row_tok, row_dst, blk_e, n_used, wgu, bgu, wd, bd, TOP_K * t + 2 * MOE_TILE)
    return moe_combine(h, g2, gate_w, slots, n_prompt_tiles, norm_f)


def _compress_kernel(npg, pt_ref, *refs):
    pages = refs[:npg]
    pe_ref, w1_ref, w2_ref, out_ref, a_ref, xs_ref = refs[npg:]
    s = pl.program_id(1)
    page = pages[0].shape[0]
    m = npg * page // CMP_STRIDE
    half = NSA_KV_HEADS * NSA_HD
    nlb = half // LANES

    @pl.when(s == 0)
    def _init():
        a_ref[0:8, :] = jnp.zeros((8, 2 * half), F32)

    for r, pg in enumerate(pages):
        for cb in range(2 * nlb):
            xs_ref[cb, r * page:(r + 1) * page, :] = pg[:, cb * LANES:(cb + 1) * LANES]

    bms = []
    for c in range(2):
        acc_a = jnp.zeros((m, half), F32)
        acc_b = jnp.zeros((m, half), F32)
        for j in range(CMP_STRIDE):
            rows = jnp.concatenate(
                [xs_ref[c * nlb + cb, pl.ds(j, m, stride=CMP_STRIDE), :] for cb in range(nlb)], axis=1)
            acc_a = acc_a + _mm(rows + pe_ref[c, j:j + 1, :], w1_ref[c, j])
            acc_b = acc_b + _mm(rows + pe_ref[c, CMP_STRIDE + j:CMP_STRIDE + j + 1, :], w1_ref[c, CMP_STRIDE + j])
        a_ref[8:8 + m, c * half:(c + 1) * half] = acc_a
        bms.append(acc_b)
    for c in range(2):
        pre = a_ref[7:7 + m, c * half:(c + 1) * half] + bms[c]
        out_ref[:, c * half:(c + 1) * half] = _mm(_silu(pre), w2_ref[c])
    a_ref[0:8, :] = a_ref[m:m + 8, :]


def compress(pages, page_table, pe_t, w1_bd, w2_bd):
    b, n_pages = page_table.shape
    npg = CMP_PAGES
    page, width = pages.shape[1:]
    half = width // 2
    m = npg * page // CMP_STRIDE
    page_specs = [pl.BlockSpec((None, page, width), functools.partial(lambda r, bi, s, pt: (pt[bi, s * npg + r], 0, 0), r))
                  for r in range(npg)]
    grid_spec = pltpu.PrefetchScalarGridSpec(
        num_scalar_prefetch=1,
        grid=(b, n_pages // npg),
        in_specs=page_specs + [pl.BlockSpec((2, CMP_BLOCK, half), lambda bi, s, pt: (0, 0, 0)),
                               pl.BlockSpec((2, CMP_BLOCK, half, half), lambda bi, s, pt: (0, 0, 0, 0)),
                               pl.BlockSpec((2, half, half), lambda bi, s, pt: (0, 0, 0))],
        out_specs=pl.BlockSpec((None, m, width), lambda bi, s, pt: (bi, s, 0)),
        scratch_shapes=[pltpu.VMEM((m + 8, width), F32),
                        pltpu.VMEM((width // LANES, npg * page, LANES), F32)],
    )
    return pl.pallas_call(
        functools.partial(_compress_kernel, npg),
        grid_spec=grid_spec,
        out_shape=jax.ShapeDtypeStruct((b, n_pages * page // CMP_STRIDE, width), F32),
        compiler_params=_cparams(("parallel", "arbitrary")),
        name="compress",
    )(page_table, *([pages] * npg), pe_t, w1_bd, w2_bd)


def _block_diag4(w):
    eye = jnp.eye(NSA_KV_HEADS, dtype=w.dtype)
    out = jnp.einsum('ij,...ab->...iajb', eye, w)
    return out.reshape(w.shape[:-2] + (NSA_KV_HEADS * w.shape[-2], NSA_KV_HEADS * w.shape[-1]))


def _cmp_softmax(s_t, slope_row, tpos_row, exp=jnp.exp):
    ncp = s_t.shape[0]
    blk = _iota((ncp, 1), 0) - 1
    c_end = blk * CMP_STRIDE + (CMP_BLOCK - 1)
    c_ctr = (blk * CMP_STRIDE).astype(F32) + 0.5 * (CMP_BLOCK - 1)
    ok = (c_end <= tpos_row) & (blk >= 0)
    s = s_t - slope_row * (tpos_row.astype(F32) - c_ctr)
    s = jnp.where(ok, s, NEG)
    mx = jnp.max(s, 0, keepdims=True)
    p = jnp.where(ok, exp(s - mx), 0.0)
    den = jnp.sum(p, 0, keepdims=True)
    return p / jnp.where(den > 0.0, den, 1.0)


def _select_blocks(imp, tpos_row, n_top):
    sj = _iota(imp.shape, 0)
    cur = tpos_row // SEL_BLOCK
    forced = (sj == 0) | (sj == cur) | (sj == cur - 1)
    score = jnp.where(sj <= cur, jnp.where(forced, SEL_FORCE, imp), -1.0)
    ns = imp.shape[0]
    sel = jnp.zeros(imp.shape, F32)
    for _ in range(n_top):
        mx = jnp.max(score, 0, keepdims=True)
        idx = jnp.min(jnp.where(score == mx, sj, ns), 0, keepdims=True)
        pick = sj == idx
        sel = jnp.where(pick & (mx >= 0.0), 1.0, sel)
        score = jnp.where(pick, -2.0, score)
    return sel


def _expand_rows(mask_rows, rep):
    r = mask_rows.shape[0]
    e = (_iota((r * rep, r), 0) // rep == _iota((r * rep, r), 1)).astype(BF16)
    return jnp.dot(e, mask_rows.astype(BF16), preferred_element_type=F32)


def _online_update(s, m_ref, l_ref):
    m_old = m_ref[...]
    m_new = jnp.maximum(m_old, jnp.max(s, 0, keepdims=True))
    alpha = jnp.exp(m_old - m_new)
    p = jnp.exp(s - m_new)
    l_ref[...] = alpha * l_ref[...] + jnp.sum(p, 0, keepdims=True)
    m_ref[...] = m_new
    return alpha, p


def _group_queries(q_ref):
    blk = q_ref[...]
    return jnp.concatenate([blk[g * NSA_HD:(g + 1) * NSA_HD, :] for g in range(NSA_GROUP)], axis=1).astype(BF16)


def _nsa_cmp_p_kernel(n_top, q_ref, kc_ref, vct_ref, selmap_ref, slope_ref, oc_ref, mask_ref):
    i = pl.program_id(1)
    qt = _group_queries(q_ref)
    lanes = qt.shape[1]
    tpos = i * Q_TILE + _iota((1, lanes), 1) % Q_TILE
    ncp = kc_ref.shape[0]

    def attend(rows):
        p = _cmp_softmax(jnp.dot(kc_ref[0:rows, :], qt, preferred_element_type=F32), slope_ref[...], tpos,
                         jnp.exp2)
        oc_ref[...] = _mm(vct_ref[:, 0:rows], p)
        imp_g = _mm_split(selmap_ref[:, 0:rows], p)
        imp = imp_g[:, 0:Q_TILE]
        for g in range(1, NSA_GROUP):
            imp = imp + imp_g[:, g * Q_TILE:(g + 1) * Q_TILE]
        mask_ref[...] = _select_blocks(imp, tpos[:, 0:Q_TILE], n_top)

    need = (i * Q_TILE + Q_TILE - CMP_BLOCK) // CMP_STRIDE + 2
    chunk = CMP_ROW_CHUNK if ncp % CMP_ROW_CHUNK == 0 else ncp
    for c in range(ncp // chunk):
        last = c == ncp // chunk - 1
        cond = (need > c * chunk) if last else ((need > c * chunk) & (need <= (c + 1) * chunk))
        pl.when(cond)(functools.partial(attend, (c + 1) * chunk))


def nsa_cmp_prompt(q_t, kc, vct, selmap_t, slopes, t):
    nkv, ncp, hd = kc.shape
    nq = t // Q_TILE
    lanes = NSA_GROUP * Q_TILE
    nsp = selmap_t.shape[0]
    n_top = min(SEL_TOP, -(-t // SEL_BLOCK))
    return pl.pallas_call(
        functools.partial(_nsa_cmp_p_kernel, n_top),
        grid=(nkv, nq),
        in_specs=[pl.BlockSpec((NSA_GROUP * hd, Q_TILE), lambda k, i: (k, i)),
                  pl.BlockSpec((None, ncp, hd), lambda k, i: (k, 0, 0)),
                  pl.BlockSpec((None, hd, ncp), lambda k, i: (k, 0, 0)),
                  pl.BlockSpec((nsp, ncp), lambda k, i: (0, 0)),
                  pl.BlockSpec((None, 1, lanes), lambda k, i: (k, 0, 0))],
        out_specs=[pl.BlockSpec((None, None, hd, lanes), lambda k, i: (k, i, 0, 0)),
                   pl.BlockSpec((None, nsp, Q_TILE), lambda k, i: (k, 0, i))],
        out_shape=[jax.ShapeDtypeStruct((nkv, nq, hd, lanes), F32),
                   jax.ShapeDtypeStruct((nkv, nsp, t), F32)],
        compiler_params=_cparams(("parallel", "parallel")),
        name="nsa_cmp_prompt",
    )(q_t, kc, vct, selmap_t, slopes)


QX_ROWS = 128
QX_DYN = NSA_HD
QX_STATIC = NSA_HD + 16
KX_ONES = NSA_HD + KV_TILE // SEL_BLOCK
VX_ROWS = NSA_HD + 16


def _split3(v):
    hi = v.astype(BF16).astype(F32)
    mid = (v - hi).astype(BF16).astype(F32)
    lo = (v - hi - mid).astype(BF16).astype(F32)
    return [hi, mid, lo]


def _nsa_slc_p_kernel(q_ref, kx_ref, vx_ref, kwx_ref, vwx_ref, mask_ref, oc_ref, gl_ref, qs_ref, slope_ref,
                      o_ref, qx_ref, m_ref, acc_ref, s_ref):
    i = pl.program_id(1)
    lanes = qx_ref.shape[1]
    t0 = i * Q_TILE
    tq = t0 + _iota((1, Q_TILE), 1)
    tq_f = (t0 + _iota((1, lanes), 1) % Q_TILE).astype(F32)
    slope = slope_ref[...]
    bpt = KV_TILE // SEL_BLOCK
    qx_ref[0:NSA_HD, :] = _group_queries(q_ref)
    qx_ref[QX_STATIC:QX_ROWS, :] = qs_ref[...]

    def set_tile_rows(j, mask_rows):
        off = _split3(slope * (jnp.asarray(j * KV_TILE).astype(F32) - tq_f))
        pad = jnp.zeros((QX_STATIC - QX_DYN - bpt - 3, lanes), F32)
        rows = jnp.concatenate([jnp.concatenate([mask_rows] * NSA_GROUP, axis=1)] + off + [pad], axis=0)
        qx_ref[QX_DYN:QX_STATIC, :] = rows.astype(BF16)

    width = lanes // 2
    halves = [slice(hf * width, (hf + 1) * width) for hf in range(2)]

    def values(v_ref, j, accs, alphas, ps):
        v_tile = v_ref[:, pl.ds(pl.multiple_of(j * KV_TILE, KV_TILE), KV_TILE)]
        return [a * acc + jnp.dot(v_tile, p, preferred_element_type=F32) for acc, a, p in zip(accs, alphas, ps)]

    def weights(j, allowed_fn):
        bias = None
        if allowed_fn is not None:
            bias = jnp.where(allowed_fn(j * KV_TILE + _iota((KV_TILE, 1), 0)), 0.0, NEG)
            bias = jnp.concatenate([bias] * (width // Q_TILE), axis=1)
        alphas, ps = [], []
        for hs in halves:
            s = s_ref[:, hs]
            if bias is not None:
                s = s + bias
            m_old = m_ref[:, hs]
            m_new = jnp.maximum(m_old, jnp.max(s, 0, keepdims=True))
            m_ref[:, hs] = m_new
            alphas.append(jnp.exp2(m_old - m_new))
            ps.append(jnp.exp2(s - m_new).astype(BF16))
        return alphas, ps

    def scores(k_ref, j, tile_mask_rows):
        set_tile_rows(j, tile_mask_rows)
        k_tile = k_ref[pl.ds(pl.multiple_of(j * KV_TILE, KV_TILE), KV_TILE), :]
        return [jnp.dot(k_tile, qx_ref[:, hs], preferred_element_type=F32) for hs in halves]

    def branch(k_ref, v_ref, first, last, mask_rows_fn, body_allowed, last_allowed):
        m_ref[...] = jnp.full(m_ref.shape, NEG, F32)
        acc_ref[...] = jnp.zeros(acc_ref.shape, F32)

        def one(j, allowed_fn):
            for hs, s in zip(halves, scores(k_ref, j, mask_rows_fn(j))):
                s_ref[:, hs] = s
            alphas, ps = weights(j, allowed_fn)
            accs = values(v_ref, j, [acc_ref[:, hs] for hs in halves], alphas, ps)
            for hs, acc in zip(halves, accs):
                acc_ref[:, hs] = acc

        def body(j, carry):
            one(j, body_allowed)
            return carry

        lax.fori_loop(first, last, body, 0)
        one(last, last_allowed)
        return acc_ref[0:NSA_HD, :] / acc_ref[NSA_HD:NSA_HD + 1, :]

    causal = lambda spos: spos <= tq
    in_window = lambda spos: (spos <= tq) & (spos > tq - WINDOW)

    def mask_bias(j):
        rows = mask_ref[pl.ds(pl.multiple_of(j * bpt, bpt), bpt), :]
        return jnp.where(rows > 0.5, 0.0, NEG)

    o_s = branch(kx_ref, vx_ref, 0, (t0 + Q_TILE + KV_TILE - 1) // KV_TILE - 1, mask_bias, None, causal)
    j_hi = t0 // KV_TILE
    o_w = branch(kwx_ref, vwx_ref, jnp.maximum(j_hi - WINDOW // KV_TILE, 0), j_hi,
                 lambda j: jnp.zeros((bpt, Q_TILE), F32), in_window, in_window)

    gates = jax.nn.sigmoid(gl_ref[...])
    o = gates[0:1, :] * oc_ref[...] + gates[1:2, :] * o_s + gates[2:3, :] * o_w
    for g in range(NSA_GROUP):
        o_ref[g * NSA_HD:(g + 1) * NSA_HD, :] = o[:, g * Q_TILE:(g + 1) * Q_TILE]


def nsa_slc_prompt(q_t, kx, vx, kwx, vwx, mask_t, oc_t, gl_t, qs, slopes):
    nkv, t, _ = kx.shape
    hd = NSA_HD
    nq = t // Q_TILE
    lanes = NSA_GROUP * Q_TILE
    nsp = mask_t.shape[1]
    return pl.pallas_call(
        _nsa_slc_p_kernel,
        grid=(nkv, nq),
        in_specs=[pl.BlockSpec((NSA_GROUP * hd, Q_TILE), lambda k, i: (k, i)),
                  pl.BlockSpec((None, t, QX_ROWS), lambda k, i: (k, 0, 0)),
                  pl.BlockSpec((None, VX_ROWS, t), lambda k, i: (k, 0, 0)),
                  pl.BlockSpec((None, t, QX_ROWS), lambda k, i: (k, 0, 0)),
                  pl.BlockSpec((None, VX_ROWS, t), lambda k, i: (k, 0, 0)),
                  pl.BlockSpec((None, nsp, Q_TILE), lambda k, i: (k, 0, i)),
                  pl.BlockSpec((None, None, hd, lanes), lambda k, i: (k, i, 0, 0)),
                  pl.BlockSpec((None, None, 3, lanes), lambda k, i: (k, i, 0, 0)),
                  pl.BlockSpec((None, QX_ROWS - QX_STATIC, lanes), lambda k, i: (k, 0, 0)),
                  pl.BlockSpec((None, 1, lanes), lambda k, i: (k, 0, 0))],
        out_specs=pl.BlockSpec((NSA_GROUP * hd, Q_TILE), lambda k, i: (k, i)),
        out_shape=jax.ShapeDtypeStruct((nkv * NSA_GROUP * hd, t), F32),
        scratch_shapes=[pltpu.VMEM((QX_ROWS, lanes), BF16),
                        pltpu.VMEM((1, lanes), F32),
                        pltpu.VMEM((VX_ROWS, lanes), F32),
                        pltpu.VMEM((KV_TILE, lanes), F32)],
        compiler_params=_cparams(("parallel", "arbitrary")),
        name="nsa_slc_prompt",
    )(q_t, kx, vx, kwx, vwx, mask_t, oc_t, gl_t, qs, slopes)


def _diag_blocks(o_full, rows_per_head):
    return jnp.concatenate(
        [o_full[k * rows_per_head:(k + 1) * rows_per_head, k * NSA_HD:(k + 1) * NSA_HD]
         for k in range(NSA_KV_HEADS)], axis=0)


def _nsa_cmp_s_kernel(n_top, qbd_ref, kc_ref, vc_ref, selmap_ref, gsum_ref, slope_ref, tpos_ref,
                      oc_ref, mask_ref):
    tpos = tpos_ref[...]
    p = _cmp_softmax(_mm(kc_ref[...], qbd_ref[...]), slope_ref[...], tpos)
    lanes = p.shape[1]
    oc_ref[...] = _diag_blocks(_mm_tn(p, vc_ref[...]), lanes // NSA_KV_HEADS)
    imp = _mm_split_l(_mm_split(selmap_ref[...], p), gsum_ref[...])
    mask_ref[...] = _select_blocks(imp, tpos, n_top)


def nsa_cmp_sample(qbd, cmp_kv, selmap_t, gsum, slope_row, tpos_row, ls):
    b, width, lanes = qbd.shape
    ncp = cmp_kv.shape[1]
    nsp = selmap_t.shape[0]
    n_top = min(SEL_TOP, -(-ls // SEL_BLOCK))
    return pl.pallas_call(
        functools.partial(_nsa_cmp_s_kernel, n_top),
        grid=(b,),
        in_specs=[pl.BlockSpec((None, width, lanes), lambda bi: (bi, 0, 0)),
                  pl.BlockSpec((None, ncp, width), lambda bi: (bi, 0, 0)),
                  pl.BlockSpec((None, ncp, width), lambda bi: (bi, 0, 1)),
                  pl.BlockSpec((nsp, ncp), lambda bi: (0, 0)),
                  pl.BlockSpec((lanes, lanes), lambda bi: (0, 0)),
                  pl.BlockSpec((1, lanes), lambda bi: (0, 0)),
                  pl.BlockSpec((1, lanes), lambda bi: (0, 0))],
        out_specs=[pl.BlockSpec((None, lanes, NSA_HD), lambda bi: (bi, 0, 0)),
                   pl.BlockSpec((None, nsp, lanes), lambda bi: (bi, 0, 0))],
        out_shape=[jax.ShapeDtypeStruct((b, lanes, NSA_HD), F32),
                   jax.ShapeDtypeStruct((b, nsp, lanes), F32)],
        compiler_params=_cparams(("parallel",)),
        name="nsa_cmp_sample",
    )(qbd, cmp_kv, cmp_kv, selmap_t, gsum, slope_row, tpos_row)


def _nsa_slc_s_kernel(npg, past, w_start, pt_ref, *refs):
    pages = refs[:npg]
    (qbd_ref, mask_ref, new_ref, win_ref, oc_ref, gl_ref, slope_ref, tpos_ref,
     o_ref, m_ref, l_ref, acc_ref) = refs[npg:]
    s = pl.program_id(1)
    ns = pl.num_programs(1)
    qbd = qbd_ref[...]
    width = qbd.shape[0]
    lanes = qbd.shape[1]
    tpos = tpos_ref[...]
    tf = tpos.astype(F32)
    slope = slope_ref[...]
    page = pages[0].shape[0]
    bpp = page // SEL_BLOCK

    @pl.when(s == 0)
    def _init():
        m_ref[...] = jnp.full(m_ref.shape, NEG, F32)
        l_ref[...] = jnp.zeros(l_ref.shape, F32)
        acc_ref[...] = jnp.zeros(acc_ref.shape, F32)

    def scores(k_rows, pos0):
        n = k_rows.shape[0]
        spos = pos0 + _iota((n, 1), 0)
        return _mm(k_rows, qbd) - slope * (tf - spos.astype(F32)), spos

    def update(sc, v_rows):
        alpha, p = _online_update(sc, m_ref, l_ref)
        acc_ref[...] = _row_to_col(alpha) * acc_ref[...] + _mm_tn(p, v_rows)

    k_rows = jnp.concatenate([pg[:, 0:width] for pg in pages], axis=0)
    v_rows = jnp.concatenate([pg[:, width:2 * width] for pg in pages], axis=0)
    sc, spos = scores(k_rows, s * (npg * page))
    nblk = npg * bpp
    rows = mask_ref[pl.ds(pl.multiple_of(s * nblk, nblk), nblk), :]
    allowed = (_expand_rows(rows, SEL_BLOCK) > 0.5) & (spos <= tpos)
    update(jnp.where(allowed, sc, NEG), v_rows)

    @pl.when(s == ns - 1)
    def _finish():
        new = new_ref[...]
        sc_n, spos_n = scores(new[:, 0:width], past)
        blk0 = past // SEL_BLOCK
        row = mask_ref[pl.ds(blk0, 8), :][0:1, :]
        ok_n = (row > 0.5) & (spos_n <= tpos)
        update(jnp.where(ok_n, sc_n, NEG), new[:, width:2 * width])
        o_s = acc_ref[...] / _row_to_col(l_ref[...])
        win = win_ref[...]
        sc_w, wpos = scores(win[:, 0:width], w_start)
        ok_w = (wpos >= 0) & (wpos <= tpos) & (wpos > tpos - WINDOW)
        sc_w = jnp.where(ok_w, sc_w, NEG)
        p_w = jnp.exp(sc_w - jnp.max(sc_w, 0, keepdims=True))
        o_w = _mm_tn(p_w, win[:, width:2 * width]) / _row_to_col(jnp.sum(p_w, 0, keepdims=True))
        rph = lanes // NSA_KV_HEADS
        gates = jax.nn.sigmoid(gl_ref[...])
        o_ref[...] = (gates[:, 0:1] * oc_ref[...] + gates[:, 1:2] * _diag_blocks(o_s, rph)
                      + gates[:, 2:3] * _diag_blocks(o_w, rph))


def nsa_slc_sample(pages, page_table, qbd, mask_t, new_rows, win_all, oc, gl, slope_row, tpos_row,
                   past, w_start):
    b, n_pages = page_table.shape
    npg = SLC_PAGES
    page, width2 = pages.shape[1:]
    width, lanes = qbd.shape[1:]
    nsp = mask_t.shape[1]
    n_new = new_rows.shape[1]
    n_win = win_all.shape[1]
    page_specs = [pl.BlockSpec((None, page, width2), functools.partial(lambda r, bi, s, pt: (pt[bi, s * npg + r], 0, 0), r))
                  for r in range(npg)]
    per_b = lambda shape: pl.BlockSpec((None,) + shape, lambda bi, s, pt: (bi, 0, 0))
    const = lambda shape: pl.BlockSpec(shape, lambda bi, s, pt: (0, 0))
    grid_spec = pltpu.PrefetchScalarGridSpec(
        num_scalar_prefetch=1,
        grid=(b, n_pages // npg),
        in_specs=page_specs + [per_b((width, lanes)), per_b((nsp, lanes)), per_b((n_new, width2)),
                               per_b((n_win, width2)), per_b((lanes, NSA_HD)), per_b((lanes, 8)),
                               const((1, lanes)), const((1, lanes))],
        out_specs=per_b((lanes, NSA_HD)),
        scratch_shapes=[pltpu.VMEM((1, lanes), F32),
                        pltpu.VMEM((1, lanes), F32),
                        pltpu.VMEM((lanes, width), F32)],
    )
    return pl.pallas_call(
        functools.partial(_nsa_slc_s_kernel, npg, past, w_start),
        grid_spec=grid_spec,
        out_shape=jax.ShapeDtypeStruct((b, lanes, NSA_HD), F32),
        compiler_params=_cparams(("parallel", "arbitrary")),
        name="nsa_slc_sample",
    )(page_table, *([pages] * npg), qbd, mask_t, new_rows, win_all, oc, gl, slope_row, tpos_row)


def _alibi_slopes():
    return 2.0 ** (-8.0 * (np.arange(NSA_HEADS) + 1) / NSA_HEADS)


def _selmap_t(nc, ns, ns_pad, ncp):
    r = CMP_BLOCK // CMP_STRIDE
    rs = SEL_BLOCK // CMP_STRIDE
    d = np.arange(nc)[:, None] - rs * np.arange(ns)[None, :]
    m = sum(((d + n >= 0) & (d + n < rs)).astype(np.float32) for n in range(r))
    out = np.zeros((ns_pad, ncp), np.float32)
    out[:ns, 1:nc + 1] = m.T
    return jnp.asarray(out, BF16)


def _mod_rows(chunk, n_sample_rep):
    top = jnp.broadcast_to(chunk[0:1], (ROW_TILE, chunk.shape[1]))
    bottom = jnp.repeat(chunk[1:], n_sample_rep, axis=0)
    return jnp.concatenate([top, bottom], 0)


def _key_extension(t):
    pos = np.arange(t)
    bpt = KV_TILE // SEL_BLOCK
    ext = np.zeros((t, QX_ROWS - NSA_HD), np.float32)
    ext[pos, (pos // SEL_BLOCK) % bpt] = 1.0
    ext[:, bpt:bpt + 3] = 1.0
    ext[:, 16:19] = (pos % LANES)[:, None]
    ext[:, 19:22] = ((pos // LANES) % (KV_TILE // LANES))[:, None]
    return jnp.asarray(ext, BF16)


def nsa_prompt(q_t, gl_t_rows, kv, v_t, cmp_kv, t):
    nkv, grp, hd = NSA_KV_HEADS, NSA_GROUP, NSA_HD
    nq = t // Q_TILE
    lanes = grp * Q_TILE
    half = nkv * hd
    log2e = float(np.log2(np.e))
    gl_t = jnp.transpose(gl_t_rows.reshape(3, nkv, grp, nq, Q_TILE), (1, 3, 0, 2, 4)).reshape(nkv, nq, 3, lanes)
    ext = jnp.broadcast_to(_key_extension(t), (nkv, t, QX_ROWS - hd))
    ones_rows = jnp.zeros((nkv, VX_ROWS - hd, t), BF16).at[:, 0, :].set(1.0)
    keys = lambda x: jnp.concatenate([jnp.transpose(x.reshape(t, nkv, hd), (1, 0, 2)).astype(BF16), ext], 2)
    vals = lambda x_t: jnp.concatenate([x_t[:, :t].reshape(nkv, hd, t).astype(BF16), ones_rows], 1)
    kx, vx = keys(kv[:, 2 * half:3 * half]), vals(v_t[0:half])
    kwx, vwx = keys(kv[:, 4 * half:5 * half]), vals(v_t[half:2 * half])
    ncp = cmp_kv.shape[1]
    kc = jnp.transpose(cmp_kv[0, :, 0:half].reshape(ncp, nkv, hd), (1, 0, 2)).astype(BF16)
    vct = jnp.transpose(cmp_kv[0, :, half:2 * half].reshape(ncp, nkv, hd), (1, 2, 0)).astype(BF16)
    ns = -(-t // SEL_BLOCK)
    nsp = -(-ns // SUBLANES) * SUBLANES
    selmap_t = _selmap_t(ncp - 1, ns, nsp, ncp)
    slopes = jnp.asarray(np.repeat(_alibi_slopes().reshape(nkv, 1, grp, 1), Q_TILE, axis=3).reshape(nkv, 1, lanes)
                         * log2e, F32)
    qs = jnp.concatenate(_split3(slopes) + _split3(slopes * LANES)
                         + [jnp.zeros((nkv, QX_ROWS - QX_STATIC - 6, lanes), F32)], 1).astype(BF16)
    oc_t, mask_t = nsa_cmp_prompt(q_t, kc, vct, selmap_t, slopes, t)
    return nsa_slc_prompt(q_t, kx, vx, kwx, vwx, mask_t, oc_t, gl_t, qs, slopes)


def nsa_sample(q, gl, kv_new, cmp_kv, slc_pages, page_table, win_all, b, l, past, w_start):
    nkv, grp, hd = NSA_KV_HEADS, NSA_GROUP, NSA_HD
    lanes = nkv * grp * l
    half = nkv * hd
    q5 = q.reshape(b, l, nkv, grp, hd)
    qbd = jnp.einsum('btkgd,kj->bjdkgt', q5, jnp.eye(nkv, dtype=F32)).reshape(b, half, lanes).astype(BF16)
    gl_r = jnp.transpose(gl.reshape(b, l, 3, nkv, grp), (0, 3, 4, 1, 2)).reshape(b, lanes, 3)
    gl_r = jnp.pad(gl_r, ((0, 0), (0, 0), (0, 5)))
    slope_row = jnp.asarray(np.repeat(_alibi_slopes(), l).reshape(1, lanes), F32)
    tpos_row = jnp.asarray(np.tile(past + np.arange(l), nkv * grp).reshape(1, lanes), I32)
    ls = past + l
    ncp = cmp_kv.shape[1]
    ns = -(-ls // SEL_BLOCK)
    nsp = -(-(ns + SUBLANES) // SUBLANES) * SUBLANES
    selmap_t = _selmap_t(ncp - 1, ns, nsp, ncp)
    lane = np.arange(lanes)
    same = (lane[:, None] // (grp * l) == lane[None, :] // (grp * l)) & (lane[:, None] % l == lane[None, :] % l)
    gsum = jnp.asarray(same.astype(np.float32), BF16)
    oc, mask_t = nsa_cmp_sample(qbd, cmp_kv, selmap_t, gsum, slope_row, tpos_row, ls)
    new_rows = kv_new[:, 2 * half:4 * half].reshape(b, l, 2 * half)
    o = nsa_slc_sample(slc_pages, page_table, qbd, mask_t, new_rows, win_all, oc, gl_r, slope_row, tpos_row,
                       past, w_start)
    return jnp.transpose(o.reshape(b, nkv, grp, l, hd), (0, 3, 1, 2, 4)).reshape(b * l, nkv * grp * hd)


def kernel(x_prompt, x_sample, state_gdn, state_conv, cache_cmp_kv, cache_slc_kv, cache_win_kv, page_table,
           c_prompt, c_sample, ada_w, ada_b, norm_mix, norm_ffn, gdn_w_in, gdn_conv_w, gdn_a_log, gdn_dt_bias,
           gdn_norm, gdn_w_out, kv_ada_w, kv_ada_b, kv_norm, kv_w, cmp_pe, cmp_w1, cmp_w2, nsa_w_in, nsa_w_out,
           router_w, router_b, moe_w_gu, moe_b_gu, moe_w_dn, moe_b_dn, norm_f):
    bp, seq, d = x_prompt.shape
    db, dl, _ = x_sample.shape
    assert bp == 1 and db * dl == ROW_TILE and seq % ROW_TILE == 0 and ada_w.shape[0] == 2
    n_pt = seq // ROW_TILE
    page = cache_cmp_kv.shape[1]
    past = page_table.shape[1] * page
    w_buf = cache_win_kv.shape[1]
    w_start = past - w_buf
    nkv, hd = NSA_KV_HEADS, NSA_HD
    half = nkv * hd
    kvw = 2 * half
    hw = GDN_HEADS * GDN_DK
    qkvw = 3 * hw

    c_all = jnp.concatenate([c_prompt, c_sample], 0)
    n_c = c_all.shape[0]
    c_pad = jnp.pad(c_all, ((0, -n_c % SUBLANES), (0, 0)))
    rows6 = lambda mod: [_mod_rows(ch, dl) for ch in jnp.split(mod[:n_c], mod.shape[1] // d, -1)]
    mod0 = rows6(cond_matmul(c_pad, ada_w[0], ada_b[0]))
    mod1 = rows6(cond_matmul(c_pad, ada_w[1], ada_b[1]))
    kv_sh, kv_sc = rows6(cond_matmul(c_pad, kv_ada_w, kv_ada_b))

    h = jnp.concatenate([x_prompt.reshape(seq, d), x_sample.reshape(db * dl, d)], 0)

    sh1, sc1, g1, sh2, sc2, g2 = mod0
    w_in = gdn_w_in[0]
    w_in = jnp.pad(w_in, ((0, 0), (0, -w_in.shape[1] % LANES))).astype(BF16)
    (proj,) = norm_proj(h, [(norm_mix[0], sh1, sc1, w_in, False)], n_pt)
    zeros_s = jnp.zeros((1,) + state_gdn.shape[2:], F32)
    zeros_c = jnp.zeros((1, 8, qkvw), F32)
    conv_s = jnp.pad(state_conv[0], ((0, 0), (8 - (GDN_CONV - 1), 0), (0, 0)))
    gdn_args = (gdn_conv_w[0], gdn_a_log[0], gdn_dt_bias[0], gdn_norm[0])
    o_p, p_gdn, p_conv = gdn(proj, 0, 1, seq, min(GDN_CHUNK, seq), zeros_s, zeros_c, *gdn_args)
    o_s, s_gdn, s_conv = gdn(proj, seq, db, dl, min(GDN_CHUNK, dl), state_gdn[0], conv_s, *gdn_args)
    h = proj_residual(jnp.concatenate([o_p, o_s], 0), gdn_w_out[0].astype(BF16), h, g1, n_pt)

    def expert_weights(layer):
        w_dn = moe_w_dn[layer]
        wd = jnp.stack([w_dn, jnp.zeros_like(w_dn)], 2).reshape(w_dn.shape[0], 2 * w_dn.shape[1], w_dn.shape[2])
        return (moe_w_gu[layer].astype(BF16), moe_b_gu[layer][:, None, :], wd.astype(BF16), moe_b_dn[layer][:, None, :])

    h = moe_layer(h, norm_ffn[0], sh2, sc2, g2, router_w[0], router_b[0], *expert_weights(0), n_pt)[0]

    sh1, sc1, g1, sh2, sc2, g2 = mod1
    nq_cols = NSA_HEADS * hd
    log2e = float(np.log2(np.e))
    w_q = nsa_w_in[0].T
    w_q = jnp.concatenate([w_q[:nq_cols] * (hd ** -0.5 * log2e), w_q[nq_cols:]], 0)
    w_q = jnp.pad(w_q, ((0, -w_q.shape[0] % LANES), (0, 0))).astype(BF16)
    w_v_t = jnp.concatenate([kv_w[:, 3 * half:4 * half], kv_w[:, 5 * half:6 * half]], 1).T.astype(BF16)
    kvp, v_t, q_t = norm_proj(h, [(kv_norm, kv_sh, kv_sc, kv_w.astype(BF16), False),
                                  (kv_norm, kv_sh, kv_sc, w_v_t, True),
                                  (norm_mix[1], sh1, sc1, w_q, True)], n_pt)
    gl_t = q_t[nq_cols:nq_cols + 3 * NSA_HEADS]
    kv_p, kv_s = kvp[:seq], kvp[seq:]
    p_cmp, p_slc, p_win = kv_p[:, 0:kvw], kv_p[:, kvw:2 * kvw], kv_p[:, 2 * kvw:3 * kvw]
    s_cmp, s_slc, s_win_new = kv_s[:, 0:kvw], kv_s[:, kvw:2 * kvw], kv_s[:, 2 * kvw:3 * kvw]

    pe_t = jnp.tile(cmp_pe, (1, 1, nkv))
    w1_bd = _block_diag4(cmp_w1).astype(BF16)
    w2_bd = _block_diag4(cmp_w2).astype(BF16)
    cmp_p = compress(p_cmp.reshape(seq // page, page, kvw), jnp.arange(seq // page, dtype=I32).reshape(1, -1),
                     pe_t, w1_bd, w2_bd)
    cmp_s = compress(cache_cmp_kv.reshape(-1, page, kvw), page_table, pe_t, w1_bd, w2_bd)

    win_all = jnp.concatenate([cache_win_kv.reshape(db, w_buf, kvw), s_win_new.reshape(db, dl, kvw)], 1)
    o_p_t = nsa_prompt(q_t, gl_t[:, :seq], kv_p, v_t, cmp_p, seq)
    q_s = q_t[:nq_cols, seq:].T * (1.0 / log2e)
    o_s = nsa_sample(q_s, gl_t[:, seq:].T, kv_s, cmp_s, cache_slc_kv.reshape(-1, page, kvw), page_table,
                     win_all, db, dl, past, w_start)
    h = proj_residual_t(o_p_t, o_s.T, nsa_w_out[0].astype(BF16), h, g1, n_pt)
    h, y = moe_layer(h, norm_ffn[1], sh2, sc2, g2, router_w[1], router_b[1], *expert_weights(1), n_pt, norm_f)

    kv5 = lambda x, b_: x.reshape(b_, -1, 2, nkv, hd)
    win_keep = lambda x: x[:, max(0, x.shape[1] - WINDOW):]
    return (y[:seq].reshape(bp, seq, d), y[seq:].reshape(db, dl, d),
            p_gdn[None], p_conv[None],
            kv5(p_cmp, bp), kv5(p_slc, bp), win_keep(kv5(p_win, bp)),
            s_gdn[None], s_conv[None],
            kv5(s_cmp, db), kv5(s_slc, db), win_keep(kv5(win_all, db)))
```

```python
import functools

import jax
import jax.numpy as jnp
import numpy as np
from jax import lax
from jax.experimental import pallas as pl
from jax.experimental.pallas import tpu as pltpu

F32 = jnp.float32
BF16 = jnp.bfloat16
I32 = jnp.int32

GDN_HEADS = 8
GDN_DK = 128
GDN_DV = 128
GDN_CONV = 4
GDN_CHUNK = 64
NSA_HEADS = 16
NSA_KV_HEADS = 4
NSA_GROUP = NSA_HEADS // NSA_KV_HEADS
NSA_HD = 64
CMP_BLOCK = 32
CMP_STRIDE = 16
SEL_BLOCK = 64
SEL_TOP = 16
SEL_FORCE = 1000.0
WINDOW = 512
TOP_K = 4
SWIGLU_LIMIT = 7.0
SWIGLU_ALPHA = 1.702
EPS = 1e-6
NEG = -1e30

LANES = 128
SUBLANES = 8
ROW_TILE = 256
MOE_TILE = 256
KV_TILE = 512
Q_TILE = 128
CMP_PAGES = 16
CMP_ROW_CHUNK = 256
SLC_PAGES = 8
VMEM_LIMIT = 56 * 1024 * 1024
HIGHEST = lax.Precision.HIGHEST


def _cparams(sem):
    return pltpu.CompilerParams(dimension_semantics=sem, vmem_limit_bytes=VMEM_LIMIT)


def _mm(a, b):
    return jnp.dot(a.astype(BF16), b.astype(BF16), preferred_element_type=F32)


def _mm_nt(a, b):
    return lax.dot_general(a.astype(BF16), b.astype(BF16), (((1,), (1,)), ((), ())),
                           preferred_element_type=F32)


def _mm_tn(a, b):
    return lax.dot_general(a.astype(BF16), b.astype(BF16), (((0,), (0,)), ((), ())),
                           preferred_element_type=F32)


def _mm32(a, b):
    return jnp.dot(a, b, precision=HIGHEST, preferred_element_type=F32)


def _mm_split(a_exact_bf16, p):
    p_hi = p.astype(BF16)
    p_lo = (p - p_hi.astype(F32)).astype(BF16)
    return (jnp.dot(a_exact_bf16, p_hi, preferred_element_type=F32)
            + jnp.dot(a_exact_bf16, p_lo, preferred_element_type=F32))


def _mm_split_l(p, b_exact_bf16):
    p_hi = p.astype(BF16)
    p_lo = (p - p_hi.astype(F32)).astype(BF16)
    return (jnp.dot(p_hi, b_exact_bf16, preferred_element_type=F32)
            + jnp.dot(p_lo, b_exact_bf16, preferred_element_type=F32))


def _row_to_col(row):
    n = row.shape[1]
    eye = _iota((n, n), 0) == _iota((n, n), 1)
    return jnp.sum(jnp.where(eye, jnp.broadcast_to(row, (n, n)), 0.0), axis=1, keepdims=True)


def _silu(x):
    return x * jax.nn.sigmoid(x)


def _iota(shape, dim):
    return lax.broadcasted_iota(I32, shape, dim)


def _cond_kernel(c_ref, w_ref, b_ref, o_ref):
    o_ref[...] = _mm(_silu(c_ref[...]), w_ref[...]) + b_ref[...]


def cond_matmul(c, w, b, tn=1024):
    m, d = c.shape
    n = w.shape[1]
    return pl.pallas_call(
        _cond_kernel,
        grid=(n // tn,),
        in_specs=[pl.BlockSpec((m, d), lambda j: (0, 0)),
                  pl.BlockSpec((d, tn), lambda j: (0, j)),
                  pl.BlockSpec((1, tn), lambda j: (0, j))],
        out_specs=pl.BlockSpec((m, tn), lambda j: (0, j)),
        out_shape=jax.ShapeDtypeStruct((m, n), F32),
        compiler_params=_cparams(("arbitrary",)),
        name="cond_matmul",
    )(c, w, b.reshape(1, n))


def _mod_map(n_prompt_tiles):
    return lambda i: (jnp.where(i < n_prompt_tiles, 0, 1), 0)


def _norm_proj_kernel(transposed, x_ref, *refs):
    n_heads = len(transposed)
    x = x_ref[...]
    xn = x * lax.rsqrt(jnp.mean(x * x, -1, keepdims=True) + EPS)
    for i in range(n_heads):
        g_ref, sh_ref, sc_ref, w_ref = refs[4 * i:4 * i + 4]
        o_ref = refs[4 * n_heads + i]
        u = (xn * g_ref[...]) * (1.0 + sc_ref[...]) + sh_ref[...]
        o_ref[...] = _mm_nt(w_ref[...], u) if transposed[i] else _mm(u, w_ref[...])


def norm_proj(h, heads, n_prompt_tiles):
    t, d = h.shape
    tm = ROW_TILE
    in_specs = [pl.BlockSpec((tm, d), lambda i: (i, 0))]
    args = [h]
    out_specs, out_shapes = [], []
    for gamma, sh, sc, w, transposed in heads:
        in_specs += [pl.BlockSpec((1, d), lambda i: (0, 0)),
                     pl.BlockSpec((tm, d), _mod_map(n_prompt_tiles)),
                     pl.BlockSpec((tm, d), _mod_map(n_prompt_tiles)),
                     pl.BlockSpec(w.shape, lambda i: (0, 0))]
        args += [gamma.reshape(1, d), sh, sc, w]
        if transposed:
            out_specs.append(pl.BlockSpec((w.shape[0], tm), lambda i: (0, i)))
            out_shapes.append(jax.ShapeDtypeStruct((w.shape[0], t), F32))
        else:
            out_specs.append(pl.BlockSpec((tm, w.shape[1]), lambda i: (i, 0)))
            out_shapes.append(jax.ShapeDtypeStruct((t, w.shape[1]), F32))
    return pl.pallas_call(
        functools.partial(_norm_proj_kernel, tuple(hd[4] for hd in heads)),
        grid=(t // tm,),
        in_specs=in_specs,
        out_specs=out_specs,
        out_shape=out_shapes,
        compiler_params=_cparams(("parallel",)),
        name="norm_proj",
    )(*args)


def _proj_res_t_kernel(n_prompt_tiles, ap_ref, as_ref, w_ref, h_ref, g_ref, o_ref):
    i = pl.program_id(0)

    @pl.when(i < n_prompt_tiles)
    def _prompt():
        o_ref[...] = h_ref[...] + g_ref[...] * _mm_tn(ap_ref[...], w_ref[...])

    @pl.when(i >= n_prompt_tiles)
    def _sample():
        o_ref[...] = h_ref[...] + g_ref[...] * _mm_tn(as_ref[...], w_ref[...])


def proj_residual_t(a_prompt_t, a_sample_t, w, h, gate_rows, n_prompt_tiles):
    t, d = h.shape
    k = w.shape[0]
    tm = ROW_TILE
    return pl.pallas_call(
        functools.partial(_proj_res_t_kernel, n_prompt_tiles),
        grid=(t // tm,),
        in_specs=[pl.BlockSpec((k, tm), lambda i: (0, jnp.minimum(i, n_prompt_tiles - 1))),
                  pl.BlockSpec((k, tm), lambda i: (0, 0)),
                  pl.BlockSpec((k, d), lambda i: (0, 0)),
                  pl.BlockSpec((tm, d), lambda i: (i, 0)),
                  pl.BlockSpec((tm, d), _mod_map(n_prompt_tiles))],
        out_specs=pl.BlockSpec((tm, d), lambda i: (i, 0)),
        out_shape=jax.ShapeDtypeStruct((t, d), F32),
        compiler_params=_cparams(("parallel",)),
        name="proj_residual_t",
    )(a_prompt_t, a_sample_t, w, h, gate_rows)


def _proj_res_kernel(a_ref, w_ref, h_ref, g_ref, o_ref):
    o_ref[...] = h_ref[...] + g_ref[...] * _mm(a_ref[...], w_ref[...])


def proj_residual(a, w, h, gate_rows, n_prompt_tiles):
    t, k = a.shape
    d = w.shape[1]
    tm = ROW_TILE
    return pl.pallas_call(
        _proj_res_kernel,
        grid=(t // tm,),
        in_specs=[pl.BlockSpec((tm, k), lambda i: (i, 0)),
                  pl.BlockSpec((k, d), lambda i: (0, 0)),
                  pl.BlockSpec((tm, d), lambda i: (i, 0)),
                  pl.BlockSpec((tm, d), _mod_map(n_prompt_tiles))],
        out_specs=pl.BlockSpec((tm, d), lambda i: (i, 0)),
        out_shape=jax.ShapeDtypeStruct((t, d), F32),
        compiler_params=_cparams(("parallel",)),
        name="proj_residual",
    )(a, w, h, gate_rows)


def _mm3(a, b):
    a_hi = a.astype(BF16)
    b_hi = b.astype(BF16)
    a_lo = (a - a_hi.astype(F32)).astype(BF16)
    b_lo = (b - b_hi.astype(F32)).astype(BF16)
    dot = lambda x, y: jnp.dot(x, y, preferred_element_type=F32)
    return dot(a_hi, b_hi) + (dot(a_hi, b_lo) + dot(a_lo, b_hi))


def _tri_inv(ms, c):
    r = _iota((c, c), 0)
    col = _iota((c, c), 1)
    eye = (r == col).astype(F32)
    same8 = (r // 8) == (col // 8)
    ds = [jnp.where(same8, m, 0.0) for m in ms]
    d2s = [_mm3(d, d) for d in ds]
    xs = [eye - d for d in ds]
    d4s = [_mm3(d2, d2) for d2 in d2s]
    xs = [x + _mm3(x, d2) for x, d2 in zip(xs, d2s)]
    xs = [x + _mm3(x, d4) for x, d4 in zip(xs, d4s)]
    size = 8
    while size < c:
        off = ((r // (2 * size)) == (col // (2 * size))) & ((r // size) != (col // size))
        ys = [_mm3(x, jnp.where(off, m, 0.0)) for x, m in zip(xs, ms)]
        xs = [x - _mm3(y, x) for x, y in zip(xs, ys)]
        size *= 2
    return xs


def _cumsum_rows(x):
    r = _iota(x.shape, 0)
    sh = 1
    while sh < x.shape[0]:
        x = x + jnp.where(r >= sh, pltpu.roll(x, sh, 0), 0.0)
        sh *= 2
    return x


def _gdn_kernel(c, q_ref, k_ref, v_ref, z_ref, ba_ref, conv0_ref, s0_ref, cw_ref, alog_ref, dtb_ref, ng_ref,
                o_ref, sfin_ref, convn_ref, xs_ref, xc_ref, st_ref):
    n = pl.program_id(1)
    nh, dk = GDN_HEADS, GDN_DK
    hw = nh * dk

    @pl.when(n == 0)
    def _init():
        xs_ref[0:8, :] = conv0_ref[...]
        st_ref[...] = s0_ref[...]

    xs_ref[8:8 + c, 0:hw] = q_ref[...]
    xs_ref[8:8 + c, hw:2 * hw] = k_ref[...]
    xs_ref[8:8 + c, 2 * hw:3 * hw] = v_ref[...]
    acc = cw_ref[3:4, :] * xs_ref[8:8 + c, :]
    for j in range(GDN_CONV - 1):
        acc = acc + cw_ref[j:j + 1, :] * xs_ref[5 + j:5 + j + c, :]
    xc_ref[...] = _silu(acc)
    convn_ref[...] = xs_ref[5 + c:8 + c, :]
    xs_ref[0:8, :] = xs_ref[c:c + 8, :]

    ba = ba_ref[...]
    beta_all = jax.nn.sigmoid(ba)
    xa = ba + dtb_ref[...]
    softplus = jnp.maximum(xa, 0.0) + jnp.log(1.0 + jnp.exp(-jnp.abs(xa)))
    g_all = -jnp.exp(alog_ref[...]) * softplus

    r = _iota((c, c), 0)
    col = _iota((c, c), 1)
    incl = r >= col
    strict = r > col
    eye = r == col
    heads = range(nh)
    qh = [xc_ref[:, h * dk:(h + 1) * dk] for h in heads]
    kh = [xc_ref[:, hw + h * dk:hw + (h + 1) * dk] for h in heads]
    vh = [xc_ref[:, 2 * hw + h * dk:2 * hw + (h + 1) * dk] for h in heads]
    qn = [x * lax.rsqrt(jnp.sum(x * x, -1, keepdims=True) + EPS) * (dk ** -0.5) for x in qh]
    kn = [x * lax.rsqrt(jnp.sum(x * x, -1, keepdims=True) + EPS) for x in kh]
    beta = [beta_all[:, h:h + 1] for h in heads]
    gc_b = [_cumsum_rows(jnp.broadcast_to(g_all[:, nh + h:nh + h + 1], (c, dk))) for h in heads]
    gc_row = [jnp.sum(jnp.where(eye, g[:, 0:c], 0.0), axis=0, keepdims=True) for g in gc_b]
    decay = [jnp.where(incl, jnp.exp(jnp.where(incl, g[:, 0:c] - gr, 0.0)), 0.0) for g, gr in zip(gc_b, gc_row)]
    kb = [k * b for k, b in zip(kn, beta)]
    vb = [v * b for v, b in zip(vh, beta)]
    kk = [_mm_nt(a, k) for a, k in zip(kb, kn)]
    qk = [_mm_nt(q, k) for q, k in zip(qn, kn)]
    t_inv = _tri_inv([jnp.where(strict, x * dcy, 0.0) for x, dcy in zip(kk, decay)], c)
    attn = [jnp.where(incl, x * dcy, 0.0) for x, dcy in zip(qk, decay)]
    egc = [jnp.exp(g) for g in gc_b]
    u = [_mm(t, v) for t, v in zip(t_inv, vb)]
    w = [_mm(t, a * e) for t, a, e in zip(t_inv, kb, egc)]
    gl_b = [g[c - 1:c, :] for g in gc_b]
    kg = [k * jnp.exp(gl - g) for k, gl, g in zip(kn, gl_b, gc_b)]
    s = [st_ref[h] for h in heads]
    ws = [_mm(a, b) for a, b in zip(w, s)]
    o1 = [_mm(q * e, b) for q, e, b in zip(qn, egc, s)]
    v_new = [a - b for a, b in zip(u, ws)]
    o2 = [_mm(a, v) for a, v in zip(attn, v_new)]
    kv = [_mm_tn(k, v) for k, v in zip(kg, v_new)]
    for h in heads:
        s_new = s[h] * jnp.exp(gl_b[h]) + kv[h]
        st_ref[h] = s_new
        sfin_ref[h] = s_new
        o = o1[h] + o2[h]
        on = o * lax.rsqrt(jnp.mean(o * o, -1, keepdims=True) + EPS) * ng_ref[...]
        o_ref[:, h * dk:(h + 1) * dk] = on * _silu(z_ref[:, h * dk:(h + 1) * dk])


def gdn(proj, row_off, b, l, c, s0, conv0, conv_w, a_log, dt_bias, norm_g):
    nh, dk = GDN_HEADS, GDN_DK
    hw = nh * dk
    nchunks = l // c
    rb0 = row_off // c
    row = lambda bi, n: rb0 + bi * nchunks + n
    gate_blk = 4 * hw // LANES
    lane_pad = jnp.zeros((LANES - 2 * nh,), F32)
    alog_row = jnp.concatenate([jnp.zeros((nh,), F32), a_log, lane_pad]).reshape(1, LANES)
    dtb_row = jnp.concatenate([jnp.zeros((nh,), F32), dt_bias, lane_pad]).reshape(1, LANES)
    return pl.pallas_call(
        functools.partial(_gdn_kernel, c),
        grid=(b, nchunks),
        in_specs=[pl.BlockSpec((c, hw), lambda bi, n: (row(bi, n), 0)),
                  pl.BlockSpec((c, hw), lambda bi, n: (row(bi, n), 1)),
                  pl.BlockSpec((c, hw), lambda bi, n: (row(bi, n), 2)),
                  pl.BlockSpec((c, hw), lambda bi, n: (row(bi, n), 3)),
                  pl.BlockSpec((c, LANES), lambda bi, n: (row(bi, n), gate_blk)),
                  pl.BlockSpec((None, 8, 3 * hw), lambda bi, n: (bi, 0, 0)),
                  pl.BlockSpec((None, nh, dk, GDN_DV), lambda bi, n: (bi, 0, 0, 0)),
                  pl.BlockSpec((GDN_CONV, 3 * hw), lambda bi, n: (0, 0)),
                  pl.BlockSpec((1, LANES), lambda bi, n: (0, 0)),
                  pl.BlockSpec((1, LANES), lambda bi, n: (0, 0)),
                  pl.BlockSpec((1, GDN_DV), lambda bi, n: (0, 0))],
        out_specs=[pl.BlockSpec((c, hw), lambda bi, n: (bi * nchunks + n, 0)),
                   pl.BlockSpec((None, nh, dk, GDN_DV), lambda bi, n: (bi, 0, 0, 0)),
                   pl.BlockSpec((None, GDN_CONV - 1, 3 * hw), lambda bi, n: (bi, 0, 0))],
        out_shape=[jax.ShapeDtypeStruct((b * l, hw), F32),
                   jax.ShapeDtypeStruct((b, nh, dk, GDN_DV), F32),
                   jax.ShapeDtypeStruct((b, GDN_CONV - 1, 3 * hw), F32)],
        scratch_shapes=[pltpu.VMEM((c + 8, 3 * hw), F32),
                        pltpu.VMEM((c, 3 * hw), F32),
                        pltpu.VMEM((nh, dk, GDN_DV), F32)],
        compiler_params=_cparams(("parallel", "arbitrary")),
        name="gdn",
    )(proj, proj, proj, proj, proj, conv0, s0, conv_w, alog_row, dtb_row, norm_g.reshape(1, GDN_DV))


def _route_kernel(x_ref, g_ref, sh_ref, sc_ref, rw_ref, rb_ref, u_ref, e_ref, w_ref):
    x = x_ref[...]
    xn = x * lax.rsqrt(jnp.mean(x * x, -1, keepdims=True) + EPS)
    u = (xn * g_ref[...]) * (1.0 + sc_ref[...]) + sh_ref[...]
    u_ref[...] = u
    logits = _mm32(u, rw_ref[...]) + rb_ref[...]
    tm, ne = logits.shape
    eidx = _iota((tm, ne), 1)
    lane = _iota((tm, LANES), 1)
    e_out = jnp.zeros((tm, LANES), I32)
    vals = []
    for k in range(TOP_K):
        m = jnp.max(logits, -1, keepdims=True)
        idx = jnp.min(jnp.where(logits == m, eidx, ne), -1, keepdims=True)
        logits = jnp.where(eidx == idx, -3e38, logits)
        vals.append(m)
        e_out = jnp.where(lane == k, idx, e_out)
    ex = [jnp.exp(v - vals[0]) for v in vals]
    den = ex[0] + ex[1] + ex[2] + ex[3]
    w_out = jnp.zeros((tm, LANES), F32)
    for k in range(TOP_K):
        w_out = jnp.where(lane == k, ex[k] / den, w_out)
    e_ref[...] = e_out
    w_ref[...] = w_out


def moe_route(h, gamma, sh, sc, router_w, router_b, n_prompt_tiles):
    t, d = h.shape
    ne = router_w.shape[1]
    tm = ROW_TILE
    return pl.pallas_call(
        _route_kernel,
        grid=(t // tm,),
        in_specs=[pl.BlockSpec((tm, d), lambda i: (i, 0)),
                  pl.BlockSpec((1, d), lambda i: (0, 0)),
                  pl.BlockSpec((tm, d), _mod_map(n_prompt_tiles)),
                  pl.BlockSpec((tm, d), _mod_map(n_prompt_tiles)),
                  pl.BlockSpec((d, ne), lambda i: (0, 0)),
                  pl.BlockSpec((1, ne), lambda i: (0, 0))],
        out_specs=[pl.BlockSpec((tm, d), lambda i: (i, 0)),
                   pl.BlockSpec((tm, LANES), lambda i: (i, 0)),
                   pl.BlockSpec((tm, LANES), lambda i: (i, 0))],
        out_shape=[jax.ShapeDtypeStruct((t, d), F32),
                   jax.ShapeDtypeStruct((t, LANES), I32),
                   jax.ShapeDtypeStruct((t, LANES), F32)],
        compiler_params=_cparams(("parallel",)),
        name="moe_route",
    )(h, gamma.reshape(1, d), sh, sc, router_w, router_b.reshape(1, ne))


def _expert_kernel(be_ref, nu_ref, tok0_ref, tokn_ref, dst_ref, x_hbm, wgu_ref, bgu_ref, wd_ref, bd_ref,
                   out_hbm, xbuf, ybuf, gsem, ssem):
    i = pl.program_id(0)
    n_used = nu_ref[0]
    slot = lax.rem(i, 2)
    tm = xbuf.shape[1]
    dump0 = out_hbm.shape[0] - 2 * tm

    def row_in(tok, s, j):
        return pltpu.make_async_copy(x_hbm.at[pl.ds(tok, 1)], xbuf.at[s, pl.ds(j, 1)], gsem.at[s])

    def row_out(dst, s, j):
        return pltpu.make_async_copy(ybuf.at[s, pl.ds(j, 1)], out_hbm.at[pl.ds(dst, 1)], ssem.at[s])

    def start_gather(tok_ref, s):
        for j in range(tm):
            row_in(tok_ref[0, j], s, j).start(priority=j % 2)

    def wait_gather(s):
        for j in range(tm):
            row_in(0, s, j).wait()

    def wait_scatter(s):
        for j in range(tm):
            row_out(0, s, j).wait()

    @pl.when(i == 0)
    def _first():
        start_gather(tok0_ref, 0)
        ybuf[...] = jnp.zeros(ybuf.shape, F32)
        for s in range(2):
            pltpu.make_async_copy(ybuf.at[s], out_hbm.at[pl.ds(dump0 + s * tm, tm)], ssem.at[s]).start()

    @pl.when(i < n_used)
    def _block():
        wait_gather(slot)
        start_gather(tokn_ref, 1 - slot)
        wait_scatter(slot)
        gu = _mm(xbuf[slot], wgu_ref[...]) + bgu_ref[...]
        parts = []
        for c in range(gu.shape[1] // LANES):
            pair = gu[:, c * LANES:(c + 1) * LANES]
            g = jnp.minimum(pair, SWIGLU_LIMIT)
            u = pltpu.roll(jnp.clip(pair, -SWIGLU_LIMIT, SWIGLU_LIMIT), LANES - 1, 1)
            parts.append((u + 1.0) * (g * jax.nn.sigmoid(SWIGLU_ALPHA * g)))
        ybuf[slot] = _mm(jnp.concatenate(parts, axis=1), wd_ref[...]) + bd_ref[...]
        for j in range(tm):
            row_out(dst_ref[0, j], slot, j).start(priority=j % 2)

        @pl.when(i == n_used - 1)
        def _drain():
            wait_gather(1 - slot)
            wait_scatter(slot)
            wait_scatter(1 - slot)


def moe_experts(u, row_tok, row_dst, blk_e, n_used, wgu, bgu, wd, bd, n_out_rows):
    t, d = u.shape
    ne, _, f2 = wgu.shape
    tm = MOE_TILE
    nb = row_tok.shape[0] // tm
    tok3 = row_tok.reshape(nb, 1, tm)
    dst3 = row_dst.reshape(nb, 1, tm)
    smem_blk = lambda imap: pl.BlockSpec((None, 1, tm), imap, memory_space=pltpu.SMEM)
    grid_spec = pltpu.PrefetchScalarGridSpec(
        num_scalar_prefetch=2,
        grid=(nb,),
        in_specs=[smem_blk(lambda i, be, nu: (i, 0, 0)),
                  smem_blk(lambda i, be, nu: (jnp.minimum(i + 1, nb - 1), 0, 0)),
                  smem_blk(lambda i, be, nu: (i, 0, 0)),
                  pl.BlockSpec(memory_space=pl.ANY),
                  pl.BlockSpec((None, d, f2), lambda i, be, nu: (be[i], 0, 0)),
                  pl.BlockSpec((None, 1, f2), lambda i, be, nu: (be[i], 0, 0)),
                  pl.BlockSpec((None, f2, d), lambda i, be, nu: (be[i], 0, 0)),
                  pl.BlockSpec((None, 1, d), lambda i, be, nu: (be[i], 0, 0))],
        out_specs=pl.BlockSpec(memory_space=pl.ANY),
        scratch_shapes=[pltpu.VMEM((2, tm, d), F32),
                        pltpu.VMEM((2, tm, d), F32),
                        pltpu.SemaphoreType.DMA((2,)),
                        pltpu.SemaphoreType.DMA((2,))],
    )
    return pl.pallas_call(
        _expert_kernel,
        grid_spec=grid_spec,
        out_shape=jax.ShapeDtypeStruct((n_out_rows, d), F32),
        compiler_params=_cparams(("arbitrary",)),
        name="moe_experts",
    )(blk_e, n_used, tok3, tok3, dst3, u, wgu, bgu, wd, bd)


def route_tables(top_e, n_exp):
    t = top_e.shape[0]
    tm = MOE_TILE
    m = t * TOP_K
    flat_e = top_e.reshape(-1)
    order = jnp.argsort(flat_e, stable=True).astype(I32)
    counts = jnp.sum((flat_e[:, None] == jnp.arange(n_exp, dtype=I32)[None, :]).astype(I32), 0)
    padded = (counts + tm - 1) // tm * tm
    start = jnp.cumsum(counts) - counts
    pend = jnp.cumsum(padded)
    pstart = pend - padded
    nb = -(-(m + n_exp * (tm - 1)) // tm)
    blk_e = jnp.minimum(jnp.searchsorted(pend, jnp.arange(nb, dtype=I32) * tm, side='right'), n_exp - 1).astype(I32)
    n_used = (pend[-1] // tm).astype(I32).reshape(1)
    per_row = lambda x: jnp.repeat(x[blk_e], tm)
    ridx = jnp.arange(nb * tm, dtype=I32)
    rank = ridx - per_row(pstart)
    valid = rank < per_row(counts)
    flat_src = order[jnp.where(valid, per_row(start) + rank, 0)]
    tok = flat_src // TOP_K
    slot = flat_src % TOP_K
    pad_dst = TOP_K * t + ((ridx // tm) % 2) * tm + ridx % tm
    row_tok = jnp.where(valid, tok, 0)
    row_dst = jnp.where(valid, slot * t + tok, pad_dst)
    return row_tok, row_dst, blk_e, n_used


def _combine_kernel(final, h_ref, g_ref, w_ref, s0_ref, s1_ref, s2_ref, s3_ref, *rest):
    w = w_ref[...]
    acc = (w[:, 0:1] * s0_ref[...] + w[:, 1:2] * s1_ref[...]
           + w[:, 2:3] * s2_ref[...] + w[:, 3:4] * s3_ref[...])
    hn = h_ref[...] + g_ref[...] * acc
    if final:
        nf_ref, o_ref, y_ref = rest
        y_ref[...] = hn * lax.rsqrt(jnp.mean(hn * hn, -1, keepdims=True) + EPS) * nf_ref[...]
    else:
        (o_ref,) = rest
    o_ref[...] = hn


def moe_combine(h, gate_rows, gate_w, slots, n_prompt_tiles, norm_f=None):
    t, d = h.shape
    tm = ROW_TILE
    nt = t // tm
    final = norm_f is not None
    in_specs = [pl.BlockSpec((tm, d), lambda i: (i, 0)),
                pl.BlockSpec((tm, d), _mod_map(n_prompt_tiles)),
                pl.BlockSpec((tm, LANES), lambda i: (i, 0))]
    in_specs += [pl.BlockSpec((tm, d), functools.partial(lambda k, i: (k * nt + i, 0), k)) for k in range(TOP_K)]
    args = [h, gate_rows, gate_w, slots, slots, slots, slots]
    out_specs = [pl.BlockSpec((tm, d), lambda i: (i, 0))]
    out_shape = [jax.ShapeDtypeStruct((t, d), F32)]
    if final:
        in_specs.append(pl.BlockSpec((1, d), lambda i: (0, 0)))
        args.append(norm_f.reshape(1, d))
        out_specs.append(pl.BlockSpec((tm, d), lambda i: (i, 0)))
        out_shape.append(jax.ShapeDtypeStruct((t, d), F32))
    return pl.pallas_call(
        functools.partial(_combine_kernel, final),
        grid=(nt,),
        in_specs=in_specs,
        out_specs=out_specs,
        out_shape=out_shape,
        compiler_params=_cparams(("parallel",)),
        name="moe_combine",
    )(*args)


def moe_layer(h, gamma, sh, sc, g2, router_w, router_b, wgu, bgu, wd, bd, n_prompt_tiles, norm_f=None):
    t = h.shape[0]
    u, top_e, gate_w = moe_route(h, gamma, sh, sc, router_w, router_b, n_prompt_tiles)
    row_tok, row_dst, blk_e, n_used = route_tables(top_e[:, :TOP_K], router_w.shape[1])
    slots = moe_experts(u, row_tok, row_dst, blk_e, n_used, wgu, bgu, wd, bd, TOP_K * t + 2 * MOE_TILE)
    return moe_combine(h, g2, gate_w, slots, n_prompt_tiles, norm_f)


def _compress_kernel(npg, pt_ref, *refs):
    pages = refs[:npg]
    pe_ref, w1_ref, w2_ref, out_ref, a_ref, xs_ref = refs[npg:]
    s = pl.program_id(1)
    page = pages[0].shape[0]
    m = npg * page // CMP_STRIDE
    half = NSA_KV_HEADS * NSA_HD
    nlb = half // LANES

    @pl.when(s == 0)
    def _init():
        a_ref[0:8, :] = jnp.zeros((8, 2 * half), F32)

    for r, pg in enumerate(pages):
        for cb in range(2 * nlb):
            xs_ref[cb, r * page:(r + 1) * page, :] = pg[:, cb * LANES:(cb + 1) * LANES]

    bms = []
    for c in range(2):
        acc_a = jnp.zeros((m, half), F32)
        acc_b = jnp.zeros((m, half), F32)
        for j in range(CMP_STRIDE):
            rows = jnp.concatenate(
                [xs_ref[c * nlb + cb, pl.ds(j, m, stride=CMP_STRIDE), :] for cb in range(nlb)], axis=1)
            acc_a = acc_a + _mm(rows + pe_ref[c, j:j + 1, :], w1_ref[c, j])
            acc_b = acc_b + _mm(rows + pe_ref[c, CMP_STRIDE + j:CMP_STRIDE + j + 1, :], w1_ref[c, CMP_STRIDE + j])
        a_ref[8:8 + m, c * half:(c + 1) * half] = acc_a
        bms.append(acc_b)
    for c in range(2):
        pre = a_ref[7:7 + m, c * half:(c + 1) * half] + bms[c]
        out_ref[:, c * half:(c + 1) * half] = _mm(_silu(pre), w2_ref[c])
    a_ref[0:8, :] = a_ref[m:m + 8, :]


def compress(pages, page_table, pe_t, w1_bd, w2_bd):
    b, n_pages = page_table.shape
    npg = CMP_PAGES
    page, width = pages.shape[1:]
    half = width // 2
    m = npg * page // CMP_STRIDE
    page_specs = [pl.BlockSpec((None, page, width), functools.partial(lambda r, bi, s, pt: (pt[bi, s * npg + r], 0, 0), r))
                  for r in range(npg)]
    grid_spec = pltpu.PrefetchScalarGridSpec(
        num_scalar_prefetch=1,
        grid=(b, n_pages // npg),
        in_specs=page_specs + [pl.BlockSpec((2, CMP_BLOCK, half), lambda bi, s, pt: (0, 0, 0)),
                               pl.BlockSpec((2, CMP_BLOCK, half, half), lambda bi, s, pt: (0, 0, 0, 0)),
                               pl.BlockSpec((2, half, half), lambda bi, s, pt: (0, 0, 0))],
        out_specs=pl.BlockSpec((None, m, width), lambda bi, s, pt: (bi, s, 0)),
        scratch_shapes=[pltpu.VMEM((m + 8, width), F32),
                        pltpu.VMEM((width // LANES, npg * page, LANES), F32)],
    )
    return pl.pallas_call(
        functools.partial(_compress_kernel, npg),
        grid_spec=grid_spec,
        out_shape=jax.ShapeDtypeStruct((b, n_pages * page // CMP_STRIDE, width), F32),
        compiler_params=_cparams(("parallel", "arbitrary")),
        name="compress",
    )(page_table, *([pages] * npg), pe_t, w1_bd, w2_bd)


def _block_diag4(w):
    eye = jnp.eye(NSA_KV_HEADS, dtype=w.dtype)
    out = jnp.einsum('ij,...ab->...iajb', eye, w)
    return out.reshape(w.shape[:-2] + (NSA_KV_HEADS * w.shape[-2], NSA_KV_HEADS * w.shape[-1]))


def _cmp_softmax(s_t, slope_row, tpos_row, exp=jnp.exp):
    ncp = s_t.shape[0]
    blk = _iota((ncp, 1), 0) - 1
    c_end = blk * CMP_STRIDE + (CMP_BLOCK - 1)
    c_ctr = (blk * CMP_STRIDE).astype(F32) + 0.5 * (CMP_BLOCK - 1)
    ok = (c_end <= tpos_row) & (blk >= 0)
    s = s_t - slope_row * (tpos_row.astype(F32) - c_ctr)
    s = jnp.where(ok, s, NEG)
    mx = jnp.max(s, 0, keepdims=True)
    p = jnp.where(ok, exp(s - mx), 0.0)
    den = jnp.sum(p, 0, keepdims=True)
    return p / jnp.where(den > 0.0, den, 1.0)


def _select_blocks(imp, tpos_row, n_top):
    sj = _iota(imp.shape, 0)
    cur = tpos_row // SEL_BLOCK
    forced = (sj == 0) | (sj == cur) | (sj == cur - 1)
    score = jnp.where(sj <= cur, jnp.where(forced, SEL_FORCE, imp), -1.0)
    ns = imp.shape[0]
    sel = jnp.zeros(imp.shape, F32)
    for _ in range(n_top):
        mx = jnp.max(score, 0, keepdims=True)
        idx = jnp.min(jnp.where(score == mx, sj, ns), 0, keepdims=True)
        pick = sj == idx
        sel = jnp.where(pick & (mx >= 0.0), 1.0, sel)
        score = jnp.where(pick, -2.0, score)
    return sel


def _expand_rows(mask_rows, rep):
    r = mask_rows.shape[0]
    e = (_iota((r * rep, r), 0) // rep == _iota((r * rep, r), 1)).astype(BF16)
    return jnp.dot(e, mask_rows.astype(BF16), preferred_element_type=F32)


def _online_update(s, m_ref, l_ref):
    m_old = m_ref[...]
    m_new = jnp.maximum(m_old, jnp.max(s, 0, keepdims=True))
    alpha = jnp.exp(m_old - m_new)
    p = jnp.exp(s - m_new)
    l_ref[...] = alpha * l_ref[...] + jnp.sum(p, 0, keepdims=True)
    m_ref[...] = m_new
    return alpha, p


def _group_queries(q_ref):
    blk = q_ref[...]
    return jnp.concatenate([blk[g * NSA_HD:(g + 1) * NSA_HD, :] for g in range(NSA_GROUP)], axis=1).astype(BF16)


def _nsa_cmp_p_kernel(n_top, q_ref, kc_ref, vct_ref, selmap_ref, slope_ref, oc_ref, mask_ref):
    i = pl.program_id(1)
    qt = _group_queries(q_ref)
    lanes = qt.shape[1]
    tpos = i * Q_TILE + _iota((1, lanes), 1) % Q_TILE
    ncp = kc_ref.shape[0]

    def attend(rows):
        p = _cmp_softmax(jnp.dot(kc_ref[0:rows, :], qt, preferred_element_type=F32), slope_ref[...], tpos,
                         jnp.exp2)
        oc_ref[...] = _mm(vct_ref[:, 0:rows], p)
        imp_g = _mm_split(selmap_ref[:, 0:rows], p)
        imp = imp_g[:, 0:Q_TILE]
        for g in range(1, NSA_GROUP):
            imp = imp + imp_g[:, g * Q_TILE:(g + 1) * Q_TILE]
        mask_ref[...] = _select_blocks(imp, tpos[:, 0:Q_TILE], n_top)

    need = (i * Q_TILE + Q_TILE - CMP_BLOCK) // CMP_STRIDE + 2
    chunk = CMP_ROW_CHUNK if ncp % CMP_ROW_CHUNK == 0 else ncp
    for c in range(ncp // chunk):
        last = c == ncp // chunk - 1
        cond = (need > c * chunk) if last else ((need > c * chunk) & (need <= (c + 1) * chunk))
        pl.when(cond)(functools.partial(attend, (c + 1) * chunk))


def nsa_cmp_prompt(q_t, kc, vct, selmap_t, slopes, t):
    nkv, ncp, hd = kc.shape
    nq = t // Q_TILE
    lanes = NSA_GROUP * Q_TILE
    nsp = selmap_t.shape[0]
    n_top = min(SEL_TOP, -(-t // SEL_BLOCK))
    return pl.pallas_call(
        functools.partial(_nsa_cmp_p_kernel, n_top),
        grid=(nkv, nq),
        in_specs=[pl.BlockSpec((NSA_GROUP * hd, Q_TILE), lambda k, i: (k, i)),
                  pl.BlockSpec((None, ncp, hd), lambda k, i: (k, 0, 0)),
                  pl.BlockSpec((None, hd, ncp), lambda k, i: (k, 0, 0)),
                  pl.BlockSpec((nsp, ncp), lambda k, i: (0, 0)),
                  pl.BlockSpec((None, 1, lanes), lambda k, i: (k, 0, 0))],
        out_specs=[pl.BlockSpec((None, None, hd, lanes), lambda k, i: (k, i, 0, 0)),
                   pl.BlockSpec((None, nsp, Q_TILE), lambda k, i: (k, 0, i))],
        out_shape=[jax.ShapeDtypeStruct((nkv, nq, hd, lanes), F32),
                   jax.ShapeDtypeStruct((nkv, nsp, t), F32)],
        compiler_params=_cparams(("parallel", "parallel")),
        name="nsa_cmp_prompt",
    )(q_t, kc, vct, selmap_t, slopes)


QX_ROWS = 128
QX_DYN = NSA_HD
QX_STATIC = NSA_HD + 16
KX_ONES = NSA_HD + KV_TILE // SEL_BLOCK
VX_ROWS = NSA_HD + 16


def _split3(v):
    hi = v.astype(BF16).astype(F32)
    mid = (v - hi).astype(BF16).astype(F32)
    lo = (v - hi - mid).astype(BF16).astype(F32)
    return [hi, mid, lo]


def _nsa_slc_p_kernel(q_ref, kx_ref, vx_ref, kwx_ref, vwx_ref, mask_ref, oc_ref, gl_ref, qs_ref, slope_ref,
                      o_ref, qx_ref, m_ref, acc_ref, s_ref):
    i = pl.program_id(1)
    lanes = qx_ref.shape[1]
    t0 = i * Q_TILE
    tq = t0 + _iota((1, Q_TILE), 1)
    tq_f = (t0 + _iota((1, lanes), 1) % Q_TILE).astype(F32)
    slope = slope_ref[...]
    bpt = KV_TILE // SEL_BLOCK
    qx_ref[0:NSA_HD, :] = _group_queries(q_ref)
    qx_ref[QX_STATIC:QX_ROWS, :] = qs_ref[...]

    def set_tile_rows(j, mask_rows):
        off = _split3(slope * (jnp.asarray(j * KV_TILE).astype(F32) - tq_f))
        pad = jnp.zeros((QX_STATIC - QX_DYN - bpt - 3, lanes), F32)
        rows = jnp.concatenate([jnp.concatenate([mask_rows] * NSA_GROUP, axis=1)] + off + [pad], axis=0)
        qx_ref[QX_DYN:QX_STATIC, :] = rows.astype(BF16)

    width = lanes // 2
    halves = [slice(hf * width, (hf + 1) * width) for hf in range(2)]

    def values(v_ref, j, accs, alphas, ps):
        v_tile = v_ref[:, pl.ds(pl.multiple_of(j * KV_TILE, KV_TILE), KV_TILE)]
        return [a * acc + jnp.dot(v_tile, p, preferred_element_type=F32) for acc, a, p in zip(accs, alphas, ps)]

    def weights(j, allowed_fn):
        bias = None
        if allowed_fn is not None:
            bias = jnp.where(allowed_fn(j * KV_TILE + _iota((KV_TILE, 1), 0)), 0.0, NEG)
            bias = jnp.concatenate([bias] * (width // Q_TILE), axis=1)
        alphas, ps = [], []
        for hs in halves:
            s = s_ref[:, hs]
            if bias is not None:
                s = s + bias
            m_old = m_ref[:, hs]
            m_new = jnp.maximum(m_old, jnp.max(s, 0, keepdims=True))
            m_ref[:, hs] = m_new
            alphas.append(jnp.exp2(m_old - m_new))
            ps.append(jnp.exp2(s - m_new).astype(BF16))
        return alphas, ps

    def scores(k_ref, j, tile_mask_rows):
        set_tile_rows(j, tile_mask_rows)
        k_tile = k_ref[pl.ds(pl.multiple_of(j * KV_TILE, KV_TILE), KV_TILE), :]
        return [jnp.dot(k_tile, qx_ref[:, hs], preferred_element_type=F32) for hs in halves]

    def branch(k_ref, v_ref, first, last, mask_rows_fn, body_allowed, last_allowed):
        m_ref[...] = jnp.full(m_ref.shape, NEG, F32)
        acc_ref[...] = jnp.zeros(acc_ref.shape, F32)

        def one(j, allowed_fn):
            for hs, s in zip(halves, scores(k_ref, j, mask_rows_fn(j))):
                s_ref[:, hs] = s
            alphas, ps = weights(j, allowed_fn)
            accs = values(v_ref, j, [acc_ref[:, hs] for hs in halves], alphas, ps)
            for hs, acc in zip(halves, accs):
                acc_ref[:, hs] = acc

        def body(j, carry):
            one(j, body_allowed)
            return carry

        lax.fori_loop(first, last, body, 0)
        one(last, last_allowed)
        return acc_ref[0:NSA_HD, :] / acc_ref[NSA_HD:NSA_HD + 1, :]

    causal = lambda spos: spos <= tq
    in_window = lambda spos: (spos <= tq) & (spos > tq - WINDOW)

    def mask_bias(j):
        rows = mask_ref[pl.ds(pl.multiple_of(j * bpt, bpt), bpt), :]
        return jnp.where(rows > 0.5, 0.0, NEG)

    o_s = branch(kx_ref, vx_ref, 0, (t0 + Q_TILE + KV_TILE - 1) // KV_TILE - 1, mask_bias, None, causal)
    j_hi = t0 // KV_TILE
    o_w = branch(kwx_ref, vwx_ref, jnp.maximum(j_hi - WINDOW // KV_TILE, 0), j_hi,
                 lambda j: jnp.zeros((bpt, Q_TILE), F32), in_window, in_window)

    gates = jax.nn.sigmoid(gl_ref[...])
    o = gates[0:1, :] * oc_ref[...] + gates[1:2, :] * o_s + gates[2:3, :] * o_w
    for g in range(NSA_GROUP):
        o_ref[g * NSA_HD:(g + 1) * NSA_HD, :] = o[:, g * Q_TILE:(g + 1) * Q_TILE]


def nsa_slc_prompt(q_t, kx, vx, kwx, vwx, mask_t, oc_t, gl_t, qs, slopes):
    nkv, t, _ = kx.shape
    hd = NSA_HD
    nq = t // Q_TILE
    lanes = NSA_GROUP * Q_TILE
    nsp = mask_t.shape[1]
    return pl.pallas_call(
        _nsa_slc_p_kernel,
        grid=(nkv, nq),
        in_specs=[pl.BlockSpec((NSA_GROUP * hd, Q_TILE), lambda k, i: (k, i)),
                  pl.BlockSpec((None, t, QX_ROWS), lambda k, i: (k, 0, 0)),
                  pl.BlockSpec((None, VX_ROWS, t), lambda k, i: (k, 0, 0)),
                  pl.BlockSpec((None, t, QX_ROWS), lambda k, i: (k, 0, 0)),
                  pl.BlockSpec((None, VX_ROWS, t), lambda k, i: (k, 0, 0)),
                  pl.BlockSpec((None, nsp, Q_TILE), lambda k, i: (k, 0, i)),
                  pl.BlockSpec((None, None, hd, lanes), lambda k, i: (k, i, 0, 0)),
                  pl.BlockSpec((None, None, 3, lanes), lambda k, i: (k, i, 0, 0)),
                  pl.BlockSpec((None, QX_ROWS - QX_STATIC, lanes), lambda k, i: (k, 0, 0)),
                  pl.BlockSpec((None, 1, lanes), lambda k, i: (k, 0, 0))],
        out_specs=pl.BlockSpec((NSA_GROUP * hd, Q_TILE), lambda k, i: (k, i)),
        out_shape=jax.ShapeDtypeStruct((nkv * NSA_GROUP * hd, t), F32),
        scratch_shapes=[pltpu.VMEM((QX_ROWS, lanes), BF16),
                        pltpu.VMEM((1, lanes), F32),
                        pltpu.VMEM((VX_ROWS, lanes), F32),
                        pltpu.VMEM((KV_TILE, lanes), F32)],
        compiler_params=_cparams(("parallel", "arbitrary")),
        name="nsa_slc_prompt",
    )(q_t, kx, vx, kwx, vwx, mask_t, oc_t, gl_t, qs, slopes)


def _diag_blocks(o_full, rows_per_head):
    return jnp.concatenate(
        [o_full[k * rows_per_head:(k + 1) * rows_per_head, k * NSA_HD:(k + 1) * NSA_HD]
         for k in range(NSA_KV_HEADS)], axis=0)


def _nsa_cmp_s_kernel(n_top, qbd_ref, kc_ref, vc_ref, selmap_ref, gsum_ref, slope_ref, tpos_ref,
                      oc_ref, mask_ref):
    tpos = tpos_ref[...]
    p = _cmp_softmax(_mm(kc_ref[...], qbd_ref[...]), slope_ref[...], tpos)
    lanes = p.shape[1]
    oc_ref[...] = _diag_blocks(_mm_tn(p, vc_ref[...]), lanes // NSA_KV_HEADS)
    imp = _mm_split_l(_mm_split(selmap_ref[...], p), gsum_ref[...])
    mask_ref[...] = _select_blocks(imp, tpos, n_top)


def nsa_cmp_sample(qbd, cmp_kv, selmap_t, gsum, slope_row, tpos_row, ls):
    b, width, lanes = qbd.shape
    ncp = cmp_kv.shape[1]
    nsp = selmap_t.shape[0]
    n_top = min(SEL_TOP, -(-ls // SEL_BLOCK))
    return pl.pallas_call(
        functools.partial(_nsa_cmp_s_kernel, n_top),
        grid=(b,),
        in_specs=[pl.BlockSpec((None, width, lanes), lambda bi: (bi, 0, 0)),
                  pl.BlockSpec((None, ncp, width), lambda bi: (bi, 0, 0)),
                  pl.BlockSpec((None, ncp, width), lambda bi: (bi, 0, 1)),
                  pl.BlockSpec((nsp, ncp), lambda bi: (0, 0)),
                  pl.BlockSpec((lanes, lanes), lambda bi: (0, 0)),
                  pl.BlockSpec((1, lanes), lambda bi: (0, 0)),
                  pl.BlockSpec((1, lanes), lambda bi: (0, 0))],
        out_specs=[pl.BlockSpec((None, lanes, NSA_HD), lambda bi: (bi, 0, 0)),
                   pl.BlockSpec((None, nsp, lanes), lambda bi: (bi, 0, 0))],
        out_shape=[jax.ShapeDtypeStruct((b, lanes, NSA_HD), F32),
                   jax.ShapeDtypeStruct((b, nsp, lanes), F32)],
        compiler_params=_cparams(("parallel",)),
        name="nsa_cmp_sample",
    )(qbd, cmp_kv, cmp_kv, selmap_t, gsum, slope_row, tpos_row)


def _nsa_slc_s_kernel(npg, past, w_start, pt_ref, *refs):
    pages = refs[:npg]
    (qbd_ref, mask_ref, new_ref, win_ref, oc_ref, gl_ref, slope_ref, tpos_ref,
     o_ref, m_ref, l_ref, acc_ref) = refs[npg:]
    s = pl.program_id(1)
    ns = pl.num_programs(1)
    qbd = qbd_ref[...]
    width = qbd.shape[0]
    lanes = qbd.shape[1]
    tpos = tpos_ref[...]
    tf = tpos.astype(F32)
    slope = slope_ref[...]
    page = pages[0].shape[0]
    bpp = page // SEL_BLOCK

    @pl.when(s == 0)
    def _init():
        m_ref[...] = jnp.full(m_ref.shape, NEG, F32)
        l_ref[...] = jnp.zeros(l_ref.shape, F32)
        acc_ref[...] = jnp.zeros(acc_ref.shape, F32)

    def scores(k_rows, pos0):
        n = k_rows.shape[0]
        spos = pos0 + _iota((n, 1), 0)
        return _mm(k_rows, qbd) - slope * (tf - spos.astype(F32)), spos

    def update(sc, v_rows):
        alpha, p = _online_update(sc, m_ref, l_ref)
        acc_ref[...] = _row_to_col(alpha) * acc_ref[...] + _mm_tn(p, v_rows)

    k_rows = jnp.concatenate([pg[:, 0:width] for pg in pages], axis=0)
    v_rows = jnp.concatenate([pg[:, width:2 * width] for pg in pages], axis=0)
    sc, spos = scores(k_rows, s * (npg * page))
    nblk = npg * bpp
    rows = mask_ref[pl.ds(pl.multiple_of(s * nblk, nblk), nblk), :]
    allowed = (_expand_rows(rows, SEL_BLOCK) > 0.5) & (spos <= tpos)
    update(jnp.where(allowed, sc, NEG), v_rows)

    @pl.when(s == ns - 1)
    def _finish():
        new = new_ref[...]
        sc_n, spos_n = scores(new[:, 0:width], past)
        blk0 = past // SEL_BLOCK
        row = mask_ref[pl.ds(blk0, 8), :][0:1, :]
        ok_n = (row > 0.5) & (spos_n <= tpos)
        update(jnp.where(ok_n, sc_n, NEG), new[:, width:2 * width])
        o_s = acc_ref[...] / _row_to_col(l_ref[...])
        win = win_ref[...]
        sc_w, wpos = scores(win[:, 0:width], w_start)
        ok_w = (wpos >= 0) & (wpos <= tpos) & (wpos > tpos - WINDOW)
        sc_w = jnp.where(ok_w, sc_w, NEG)
        p_w = jnp.exp(sc_w - jnp.max(sc_w, 0, keepdims=True))
        o_w = _mm_tn(p_w, win[:, width:2 * width]) / _row_to_col(jnp.sum(p_w, 0, keepdims=True))
        rph = lanes // NSA_KV_HEADS
        gates = jax.nn.sigmoid(gl_ref[...])
        o_ref[...] = (gates[:, 0:1] * oc_ref[...] + gates[:, 1:2] * _diag_blocks(o_s, rph)
                      + gates[:, 2:3] * _diag_blocks(o_w, rph))


def nsa_slc_sample(pages, page_table, qbd, mask_t, new_rows, win_all, oc, gl, slope_row, tpos_row,
                   past, w_start):
    b, n_pages = page_table.shape
    npg = SLC_PAGES
    page, width2 = pages.shape[1:]
    width, lanes = qbd.shape[1:]
    nsp = mask_t.shape[1]
    n_new = new_rows.shape[1]
    n_win = win_all.shape[1]
    page_specs = [pl.BlockSpec((None, page, width2), functools.partial(lambda r, bi, s, pt: (pt[bi, s * npg + r], 0, 0), r))
                  for r in range(npg)]
    per_b = lambda shape: pl.BlockSpec((None,) + shape, lambda bi, s, pt: (bi, 0, 0))
    const = lambda shape: pl.BlockSpec(shape, lambda bi, s, pt: (0, 0))
    grid_spec = pltpu.PrefetchScalarGridSpec(
        num_scalar_prefetch=1,
        grid=(b, n_pages // npg),
        in_specs=page_specs + [per_b((width, lanes)), per_b((nsp, lanes)), per_b((n_new, width2)),
                               per_b((n_win, width2)), per_b((lanes, NSA_HD)), per_b((lanes, 8)),
                               const((1, lanes)), const((1, lanes))],
        out_specs=per_b((lanes, NSA_HD)),
        scratch_shapes=[pltpu.VMEM((1, lanes), F32),
                        pltpu.VMEM((1, lanes), F32),
                        pltpu.VMEM((lanes, width), F32)],
    )
    return pl.pallas_call(
        functools.partial(_nsa_slc_s_kernel, npg, past, w_start),
        grid_spec=grid_spec,
        out_shape=jax.ShapeDtypeStruct((b, lanes, NSA_HD), F32),
        compiler_params=_cparams(("parallel", "arbitrary")),
        name="nsa_slc_sample",
    )(page_table, *([pages] * npg), qbd, mask_t, new_rows, win_all, oc, gl, slope_row, tpos_row)


def _alibi_slopes():
    return 2.0 ** (-8.0 * (np.arange(NSA_HEADS) + 1) / NSA_HEADS)


def _selmap_t(nc, ns, ns_pad, ncp):
    r = CMP_BLOCK // CMP_STRIDE
    rs = SEL_BLOCK // CMP_STRIDE
    d = np.arange(nc)[:, None] - rs * np.arange(ns)[None, :]
    m = sum(((d + n >= 0) & (d + n < rs)).astype(np.float32) for n in range(r))
    out = np.zeros((ns_pad, ncp), np.float32)
    out[:ns, 1:nc + 1] = m.T
    return jnp.asarray(out, BF16)


def _mod_rows(chunk, n_sample_rep):
    top = jnp.broadcast_to(chunk[0:1], (ROW_TILE, chunk.shape[1]))
    bottom = jnp.repeat(chunk[1:], n_sample_rep, axis=0)
    return jnp.concatenate([top, bottom], 0)


def _key_extension(t):
    pos = np.arange(t)
    bpt = KV_TILE // SEL_BLOCK
    ext = np.zeros((t, QX_ROWS - NSA_HD), np.float32)
    ext[pos, (pos // SEL_BLOCK) % bpt] = 1.0
    ext[:, bpt:bpt + 3] = 1.0
    ext[:, 16:19] = (pos % LANES)[:, None]
    ext[:, 19:22] = ((pos // LANES) % (KV_TILE // LANES))[:, None]
    return jnp.asarray(ext, BF16)


def nsa_prompt(q_t, gl_t_rows, kv, v_t, cmp_kv, t):
    nkv, grp, hd = NSA_KV_HEADS, NSA_GROUP, NSA_HD
    nq = t // Q_TILE
    lanes = grp * Q_TILE
    half = nkv * hd
    log2e = float(np.log2(np.e))
    gl_t = jnp.transpose(gl_t_rows.reshape(3, nkv, grp, nq, Q_TILE), (1, 3, 0, 2, 4)).reshape(nkv, nq, 3, lanes)
    ext = jnp.broadcast_to(_key_extension(t), (nkv, t, QX_ROWS - hd))
    ones_rows = jnp.zeros((nkv, VX_ROWS - hd, t), BF16).at[:, 0, :].set(1.0)
    keys = lambda x: jnp.concatenate([jnp.transpose(x.reshape(t, nkv, hd), (1, 0, 2)).astype(BF16), ext], 2)
    vals = lambda x_t: jnp.concatenate([x_t[:, :t].reshape(nkv, hd, t).astype(BF16), ones_rows], 1)
    kx, vx = keys(kv[:, 2 * half:3 * half]), vals(v_t[0:half])
    kwx, vwx = keys(kv[:, 4 * half:5 * half]), vals(v_t[half:2 * half])
    ncp = cmp_kv.shape[1]
    kc = jnp.transpose(cmp_kv[0, :, 0:half].reshape(ncp, nkv, hd), (1, 0, 2)).astype(BF16)
    vct = jnp.transpose(cmp_kv[0, :, half:2 * half].reshape(ncp, nkv, hd), (1, 2, 0)).astype(BF16)
    ns = -(-t // SEL_BLOCK)
    nsp = -(-ns // SUBLANES) * SUBLANES
    selmap_t = _selmap_t(ncp - 1, ns, nsp, ncp)
    slopes = jnp.asarray(np.repeat(_alibi_slopes().reshape(nkv, 1, grp, 1), Q_TILE, axis=3).reshape(nkv, 1, lanes)
                         * log2e, F32)
    qs = jnp.concatenate(_split3(slopes) + _split3(slopes * LANES)
                         + [jnp.zeros((nkv, QX_ROWS - QX_STATIC - 6, lanes), F32)], 1).astype(BF16)
    oc_t, mask_t = nsa_cmp_prompt(q_t, kc, vct, selmap_t, slopes, t)
    return nsa_slc_prompt(q_t, kx, vx, kwx, vwx, mask_t, oc_t, gl_t, qs, slopes)


def nsa_sample(q, gl, kv_new, cmp_kv, slc_pages, page_table, win_all, b, l, past, w_start):
    nkv, grp, hd = NSA_KV_HEADS, NSA_GROUP, NSA_HD
    lanes = nkv * grp * l
    half = nkv * hd
    q5 = q.reshape(b, l, nkv, grp, hd)
    qbd = jnp.einsum('btkgd,kj->bjdkgt', q5, jnp.eye(nkv, dtype=F32)).reshape(b, half, lanes).astype(BF16)
    gl_r = jnp.transpose(gl.reshape(b, l, 3, nkv, grp), (0, 3, 4, 1, 2)).reshape(b, lanes, 3)
    gl_r = jnp.pad(gl_r, ((0, 0), (0, 0), (0, 5)))
    slope_row = jnp.asarray(np.repeat(_alibi_slopes(), l).reshape(1, lanes), F32)
    tpos_row = jnp.asarray(np.tile(past + np.arange(l), nkv * grp).reshape(1, lanes), I32)
    ls = past + l
    ncp = cmp_kv.shape[1]
    ns = -(-ls // SEL_BLOCK)
    nsp = -(-(ns + SUBLANES) // SUBLANES) * SUBLANES
    selmap_t = _selmap_t(ncp - 1, ns, nsp, ncp)
    lane = np.arange(lanes)
    same = (lane[:, None] // (grp * l) == lane[None, :] // (grp * l)) & (lane[:, None] % l == lane[None, :] % l)
    gsum = jnp.asarray(same.astype(np.float32), BF16)
    oc, mask_t = nsa_cmp_sample(qbd, cmp_kv, selmap_t, gsum, slope_row, tpos_row, ls)
    new_rows = kv_new[:, 2 * half:4 * half].reshape(b, l, 2 * half)
    o = nsa_slc_sample(slc_pages, page_table, qbd, mask_t, new_rows, win_all, oc, gl_r, slope_row, tpos_row,
                       past, w_start)
    return jnp.transpose(o.reshape(b, nkv, grp, l, hd), (0, 3, 1, 2, 4)).reshape(b * l, nkv * grp * hd)


def kernel(x_prompt, x_sample, state_gdn, state_conv, cache_cmp_kv, cache_slc_kv, cache_win_kv, page_table,
           c_prompt, c_sample, ada_w, ada_b, norm_mix, norm_ffn, gdn_w_in, gdn_conv_w, gdn_a_log, gdn_dt_bias,
           gdn_norm, gdn_w_out, kv_ada_w, kv_ada_b, kv_norm, kv_w, cmp_pe, cmp_w1, cmp_w2, nsa_w_in, nsa_w_out,
           router_w, router_b, moe_w_gu, moe_b_gu, moe_w_dn, moe_b_dn, norm_f):
    bp, seq, d = x_prompt.shape
    db, dl, _ = x_sample.shape
    assert bp == 1 and db * dl == ROW_TILE and seq % ROW_TILE == 0 and ada_w.shape[0] == 2
    n_pt = seq // ROW_TILE
    page = cache_cmp_kv.shape[1]
    past = page_table.shape[1] * page
    w_buf = cache_win_kv.shape[1]
    w_start = past - w_buf
    nkv, hd = NSA_KV_HEADS, NSA_HD
    half = nkv * hd
    kvw = 2 * half
    hw = GDN_HEADS * GDN_DK
    qkvw = 3 * hw

    c_all = jnp.concatenate([c_prompt, c_sample], 0)
    n_c = c_all.shape[0]
    c_pad = jnp.pad(c_all, ((0, -n_c % SUBLANES), (0, 0)))
    rows6 = lambda mod: [_mod_rows(ch, dl) for ch in jnp.split(mod[:n_c], mod.shape[1] // d, -1)]
    mod0 = rows6(cond_matmul(c_pad, ada_w[0], ada_b[0]))
    mod1 = rows6(cond_matmul(c_pad, ada_w[1], ada_b[1]))
    kv_sh, kv_sc = rows6(cond_matmul(c_pad, kv_ada_w, kv_ada_b))

    h = jnp.concatenate([x_prompt.reshape(seq, d), x_sample.reshape(db * dl, d)], 0)

    sh1, sc1, g1, sh2, sc2, g2 = mod0
    w_in = gdn_w_in[0]
    w_in = jnp.pad(w_in, ((0, 0), (0, -w_in.shape[1] % LANES))).astype(BF16)
    (proj,) = norm_proj(h, [(norm_mix[0], sh1, sc1, w_in, False)], n_pt)
    zeros_s = jnp.zeros((1,) + state_gdn.shape[2:], F32)
    zeros_c = jnp.zeros((1, 8, qkvw), F32)
    conv_s = jnp.pad(state_conv[0], ((0, 0), (8 - (GDN_CONV - 1), 0), (0, 0)))
    gdn_args = (gdn_conv_w[0], gdn_a_log[0], gdn_dt_bias[0], gdn_norm[0])
    o_p, p_gdn, p_conv = gdn(proj, 0, 1, seq, min(GDN_CHUNK, seq), zeros_s, zeros_c, *gdn_args)
    o_s, s_gdn, s_conv = gdn(proj, seq, db, dl, min(GDN_CHUNK, dl), state_gdn[0], conv_s, *gdn_args)
    h = proj_residual(jnp.concatenate([o_p, o_s], 0), gdn_w_out[0].astype(BF16), h, g1, n_pt)

    def expert_weights(layer):
        w_dn = moe_w_dn[layer].astype(BF16)
        wd = jnp.stack([w_dn, jnp.zeros_like(w_dn)], 2).reshape(w_dn.shape[0], 2 * w_dn.shape[1], w_dn.shape[2])
        return (moe_w_gu[layer].astype(BF16), moe_b_gu[layer][:, None, :], wd, moe_b_dn[layer][:, None, :])

    h = moe_layer(h, norm_ffn[0], sh2, sc2, g2, router_w[0], router_b[0], *expert_weights(0), n_pt)[0]

    sh1, sc1, g1, sh2, sc2, g2 = mod1
    nq_cols = NSA_HEADS * hd
    log2e = float(np.log2(np.e))
    w_q = nsa_w_in[0].T
    w_q = jnp.concatenate([w_q[:nq_cols] * (hd ** -0.5 * log2e), w_q[nq_cols:]], 0)
    w_q = jnp.pad(w_q, ((0, -w_q.shape[0] % LANES), (0, 0))).astype(BF16)
    w_v_t = jnp.concatenate([kv_w[:, 3 * half:4 * half], kv_w[:, 5 * half:6 * half]], 1).T.astype(BF16)
    kvp, v_t, q_t = norm_proj(h, [(kv_norm, kv_sh, kv_sc, kv_w.astype(BF16), False),
                                  (kv_norm, kv_sh, kv_sc, w_v_t, True),
                                  (norm_mix[1], sh1, sc1, w_q, True)], n_pt)
    gl_t = q_t[nq_cols:nq_cols + 3 * NSA_HEADS]
    kv_p, kv_s = kvp[:seq], kvp[seq:]
    p_cmp, p_slc, p_win = kv_p[:, 0:kvw], kv_p[:, kvw:2 * kvw], kv_p[:, 2 * kvw:3 * kvw]
    s_cmp, s_slc, s_win_new = kv_s[:, 0:kvw], kv_s[:, kvw:2 * kvw], kv_s[:, 2 * kvw:3 * kvw]

    pe_t = jnp.tile(cmp_pe, (1, 1, nkv))
    w1_bd = _block_diag4(cmp_w1).astype(BF16)
    w2_bd = _block_diag4(cmp_w2).astype(BF16)
    cmp_p = compress(p_cmp.reshape(seq // page, page, kvw), jnp.arange(seq // page, dtype=I32).reshape(1, -1),
                     pe_t, w1_bd, w2_bd)
    cmp_s = compress(cache_cmp_kv.reshape(-1, page, kvw), page_table, pe_t, w1_bd, w2_bd)

    win_all = jnp.concatenate([cache_win_kv.reshape(db, w_buf, kvw), s_win_new.reshape(db, dl, kvw)], 1)
    o_p_t = nsa_prompt(q_t, gl_t[:, :seq], kv_p, v_t, cmp_p, seq)
    q_s = q_t[:nq_cols, seq:].T * (1.0 / log2e)
    o_s = nsa_sample(q_s, gl_t[:, seq:].T, kv_s, cmp_s, cache_slc_kv.reshape(-1, page, kvw), page_table,
                     win_all, db, dl, past, w_start)
    h = proj_residual_t(o_p_t, o_s.T, nsa_w_out[0].astype(BF16), h, g1, n_pt)
    h, y = moe_layer(h, norm_ffn[1], sh2, sc2, g2, router_w[1], router_b[1], *expert_weights(1), n_pt, norm_f)

    kv5 = lambda x, b_: x.reshape(b_, -1, 2, nkv, hd)
    win_keep = lambda x: x[:, max(0, x.shape[1] - WINDOW):]
    return (y[:seq].reshape(bp, seq, d), y[seq:].reshape(db, dl, d),
            p_gdn[None], p_conv[None],
            kv5(p_cmp, bp), kv5(p_slc, bp), win_keep(kv5(p_win, bp)),
            s_gdn[None], s_conv[None],
            kv5(s_cmp, db), kv5(s_slc, db), win_keep(kv5(win_all, db)))
```
